```python
import jax, jax.numpy as jnp
from jax import lax
import numpy as np

D_MODEL = 2048
BATCH = 4
SEQ = 2048
DEPTH = 4
DEC_BATCH = 128
DEC_SEQ = 4
PAST_LEN = 8192
PAGE_SIZE = 128

N_MEM = 256
N_MLA_HEADS = 8
QK_NOPE = 128
QK_ROPE = 64
V_HEAD = 128
Q_LORA = 512
KV_LORA = 512
MLA_WIDTH = N_MLA_HEADS * V_HEAD
MLA_SCALE = (QK_NOPE + QK_ROPE) ** -0.5
Q_BLOCK = 128
N_RET_HEADS = 4
RET_HEAD = 256
RET_WIDTH = N_RET_HEADS * RET_HEAD
RET_CHUNK = 128
N_MEM_HEADS = 4
MEM_HEAD = 256
MEM_WIDTH = N_MEM_HEADS * MEM_HEAD
D_FF = 4 * D_MODEL
DEEPNORM_ALPHA = (2 * DEPTH) ** 0.25
DEEPNORM_BETA = (8 * DEPTH) ** -0.25
LN_EPS = 1e-5
RMS_EPS = 1e-6
ROPE_BASE = 10000.0
N_BRANCH = 3
OFF_QD = 0
OFF_KVD = OFF_QD + Q_LORA
OFF_RET = OFF_KVD + KV_LORA + QK_ROPE
OFF_MEMQ = OFF_RET + 4 * RET_WIDTH
OFF_GATE = OFF_MEMQ + MEM_WIDTH
IN_COLS = OFF_GATE + N_BRANCH * D_MODEL

kernel_name = "hybrid_mla_retention_memory_decoder_step"


def rms_norm(x, g):
    xf = x.astype(jnp.float32)
    y = xf * lax.rsqrt(jnp.mean(xf * xf, axis=-1, keepdims=True) + RMS_EPS)
    return (y * g.astype(jnp.float32)).astype(x.dtype)


def layer_norm(x, g, b):
    xf = x.astype(jnp.float32)
    mu = jnp.mean(xf, axis=-1, keepdims=True)
    var = jnp.mean(jnp.square(xf - mu), axis=-1, keepdims=True)
    y = (xf - mu) * lax.rsqrt(var + LN_EPS) * g.astype(jnp.float32) + b.astype(jnp.float32)
    return y.astype(x.dtype)


def rope(x, pos):
    d = x.shape[-1]
    half = d // 2
    inv = jnp.power(ROPE_BASE, -jnp.arange(half, dtype=jnp.float32) * (2.0 / d))
    ang = pos.astype(jnp.float32)[:, None] * inv[None, :]
    cos = jnp.cos(ang)[None, :, None, :].astype(x.dtype)
    sin = jnp.sin(ang)[None, :, None, :].astype(x.dtype)
    x1, x2 = x[..., :half], x[..., half:]
    return jnp.concatenate([x1 * cos - x2 * sin, x2 * cos + x1 * sin], axis=-1)


def mixer_inputs(x, pos, w_in, q_norm_g, kv_norm_g, w_uq, w_uk):
    B, T, _ = x.shape
    z = x @ w_in
    c_q = rms_norm(z[..., OFF_QD:OFF_KVD], q_norm_g)
    q = (c_q @ w_uq).reshape(B, T, N_MLA_HEADS, QK_NOPE + QK_ROPE)
    q_rope = rope(q[..., QK_NOPE:], pos)
    q_lat = jnp.einsum('bthd,chd->bthc', q[..., :QK_NOPE], w_uk)
    c_kv = rms_norm(z[..., OFF_KVD:OFF_KVD + KV_LORA], kv_norm_g)
    k_rope = rope(z[..., OFF_KVD + KV_LORA:OFF_RET][:, :, None, :], pos)[:, :, 0, :]
    ret = z[..., OFF_RET:OFF_MEMQ].reshape(B, T, 4, N_RET_HEADS, RET_HEAD)
    rq = rope(ret[:, :, 0], pos)
    rk = rope(ret[:, :, 1], pos) * (RET_HEAD ** -0.5)
    rv = ret[:, :, 2]
    rg = ret[:, :, 3].reshape(B, T, RET_WIDTH)
    mq = z[..., OFF_MEMQ:OFF_GATE].reshape(B, T, N_MEM_HEADS, MEM_HEAD)
    gates = jax.nn.sigmoid(z[..., OFF_GATE:].reshape(B, T, N_BRANCH, D_MODEL))
    return q_lat, q_rope, c_kv, k_rope, rq, rk, rv, rg, mq, gates


def mla_prompt(q_lat, q_rope, c_kv, k_rope):
    B, T, H, _ = q_lat.shape
    nb = T // Q_BLOCK
    ql = q_lat.reshape(B, nb, Q_BLOCK, H, KV_LORA).swapaxes(0, 1)
    qr = q_rope.reshape(B, nb, Q_BLOCK, H, QK_ROPE).swapaxes(0, 1)
    k_pos = jnp.arange(T)

    def block(args):
        qlb, qrb, start = args
        s = (jnp.einsum('bqhc,bkc->bhqk', qlb, c_kv)
             + jnp.einsum('bqhr,bkr->bhqk', qrb, k_rope)).astype(jnp.float32) * MLA_SCALE
        q_pos = start + jnp.arange(Q_BLOCK)
        s = jnp.where(k_pos[None, :] <= q_pos[:, None], s, -jnp.inf)
        p = jax.nn.softmax(s, axis=-1).astype(c_kv.dtype)
        return jnp.einsum('bhqk,bkc->bqhc', p, c_kv)

    o = lax.map(block, (ql, qr, jnp.arange(nb) * Q_BLOCK))
    return o.swapaxes(0, 1).reshape(B, T, H, KV_LORA)


def mla_sample(q_lat, q_rope, c_new, kr_new, c_past, kr_past):
    T = q_lat.shape[1]
    P = c_past.shape[1]
    s_p = (jnp.einsum('bqhc,bkc->bhqk', q_lat, c_past)
           + jnp.einsum('bqhr,bkr->bhqk', q_rope, kr_past)).astype(jnp.float32) * MLA_SCALE
    s_n = (jnp.einsum('bqhc,bkc->bhqk', q_lat, c_new)
           + jnp.einsum('bqhr,bkr->bhqk', q_rope, kr_new)).astype(jnp.float32) * MLA_SCALE
    causal = jnp.arange(T)[None, :] <= jnp.arange(T)[:, None]
    s_n = jnp.where(causal, s_n, -jnp.inf)
    p = jax.nn.softmax(jnp.concatenate([s_p, s_n], axis=-1), axis=-1).astype(c_new.dtype)
    return (jnp.einsum('bhqk,bkc->bqhc', p[..., :P], c_past)
            + jnp.einsum('bhqk,bkc->bqhc', p[..., P:], c_new))


def retention_chunk(S, q, k, v, log_gamma):
    C = q.shape[2]
    idx = jnp.arange(C, dtype=jnp.float32)
    lg = log_gamma[:, None]
    diff = idx[:, None] - idx[None, :]
    intra = jnp.where(diff >= 0, jnp.exp(lg[:, :, None] * jnp.maximum(diff, 0.0)), 0.0).astype(q.dtype)
    q_decay = jnp.exp(lg * (idx + 1.0)).astype(q.dtype)
    k_decay = jnp.exp(lg * (C - 1.0 - idx)).astype(q.dtype)
    chunk_decay = jnp.exp(log_gamma * C).astype(q.dtype)
    scores = jnp.einsum('bhid,bhjd->bhij', q, k) * intra
    o = (jnp.einsum('bhij,bhje->bhie', scores, v)
         + jnp.einsum('bhid,bhde->bhie', q, S) * q_decay[..., None])
    S_new = S * chunk_decay[:, None, None] + jnp.einsum('bhjd,bhje->bhde', k * k_decay[..., None], v)
    return S_new, o


def retention_prompt(rq, rk, rv, log_gamma):
    B, T, H, d = rq.shape
    nc = T // RET_CHUNK

    def to_chunks(a):
        return a.reshape(B, nc, RET_CHUNK, H, d).transpose(1, 0, 3, 2, 4)

    def step(S, qkv):
        qc, kc, vc = qkv
        return retention_chunk(S, qc, kc, vc, log_gamma)

    S0 = jnp.zeros((B, H, d, d), rq.dtype)
    S, o = lax.scan(step, S0, (to_chunks(rq), to_chunks(rk), to_chunks(rv)))
    return o.transpose(1, 0, 3, 2, 4).reshape(B, T, H, d), S


def mem_attention(mq, mk, mv):
    B, T = mq.shape[:2]
    k = mk.reshape(B, -1, N_MEM_HEADS, MEM_HEAD)
    v = mv.reshape(B, -1, N_MEM_HEADS, MEM_HEAD)
    s = jnp.einsum('bqhd,bkhd->bhqk', mq, k).astype(jnp.float32) * (MEM_HEAD ** -0.5)
    p = jax.nn.softmax(s, axis=-1).astype(v.dtype)
    return jnp.einsum('bhqk,bkhd->bqhd', p, v).reshape(B, T, MEM_WIDTH)


def merge_branches(gates, o_lat, o_ret, rg, o_mem, w_uv, ret_gn_g, ret_gn_b,
                   w_br_mla, w_br_ret, w_br_mem, w_o):
    B, T = o_lat.shape[:2]
    y_mla = jnp.einsum('bthc,chd->bthd', o_lat, w_uv).reshape(B, T, MLA_WIDTH)
    of = o_ret.astype(jnp.float32)
    mu = jnp.mean(of, axis=-1, keepdims=True)
    var = jnp.mean(jnp.square(of - mu), axis=-1, keepdims=True)
    gn = ((of - mu) * lax.rsqrt(var + LN_EPS)).reshape(B, T, RET_WIDTH)
    gn = (gn * ret_gn_g.astype(jnp.float32) + ret_gn_b.astype(jnp.float32)).astype(o_ret.dtype)
    y_ret = jax.nn.silu(rg) * gn
    merged = (gates[:, :, 0] * (y_mla @ w_br_mla)
              + gates[:, :, 1] * (y_ret @ w_br_ret)
              + gates[:, :, 2] * (o_mem @ w_br_mem))
    return merged @ w_o


def post_mixer(x, a, ln1_g, ln1_b, w_up, w_down, ln2_g, ln2_b):
    x = layer_norm(DEEPNORM_ALPHA * x + a, ln1_g, ln1_b)
    h = jnp.square(jax.nn.relu(x @ w_up))
    return layer_norm(DEEPNORM_ALPHA * x + h @ w_down, ln2_g, ln2_b)


def setup_inputs(seed: int = 0) -> dict:
    key = jax.random.key(seed)
    ks = jax.random.split(key, 32)
    n_pages = PAST_LEN // PAGE_SIZE
    n_pool = (DEC_BATCH * n_pages * 5) // 4
    f32 = jnp.float32

    def nrm(k, shape, scale):
        return jax.random.normal(k, shape, f32) * scale

    page_table = jax.random.permutation(ks[5], n_pool)[: DEC_BATCH * n_pages]
    page_table = page_table.reshape(DEC_BATCH, n_pages).astype(jnp.int32)
    b = DEEPNORM_BETA
    return {
        "x_prompt": nrm(ks[0], (BATCH, SEQ, D_MODEL), 1.0),
        "x_sample": nrm(ks[1], (DEC_BATCH, DEC_SEQ, D_MODEL), 1.0),
        "mem_prompt": nrm(ks[2], (BATCH, N_MEM, D_MODEL), 1.0),
        "cache_mla_latent": nrm(ks[3], (DEPTH, n_pool, PAGE_SIZE, KV_LORA), 1.0),
        "cache_mla_rope": nrm(ks[4], (DEPTH, n_pool, PAGE_SIZE, QK_ROPE), 1.0),
        "page_table": page_table,
        "state_ret": nrm(ks[6], (DEPTH, DEC_BATCH, N_RET_HEADS, RET_HEAD, RET_HEAD), 0.5),
        "cache_mem_k": nrm(ks[7], (DEPTH, DEC_BATCH, N_MEM, MEM_WIDTH), 1.0),
        "cache_mem_v": nrm(ks[8], (DEPTH, DEC_BATCH, N_MEM, MEM_WIDTH), b),
        "w_in": nrm(ks[9], (DEPTH, D_MODEL, IN_COLS), D_MODEL ** -0.5),
        "q_norm_g": 1.0 + nrm(ks[10], (DEPTH, Q_LORA), 0.02),
        "kv_norm_g": 1.0 + nrm(ks[11], (DEPTH, KV_LORA), 0.02),
        "w_uq": nrm(ks[12], (DEPTH, Q_LORA, N_MLA_HEADS * (QK_NOPE + QK_ROPE)), Q_LORA ** -0.5),
        "w_uk": nrm(ks[13], (DEPTH, KV_LORA, N_MLA_HEADS, QK_NOPE), KV_LORA ** -0.5),
        "w_uv": nrm(ks[14], (DEPTH, KV_LORA, N_MLA_HEADS, V_HEAD), b * KV_LORA ** -0.5),
        "ret_gn_g": 1.0 + nrm(ks[15], (DEPTH, RET_WIDTH), 0.02),
        "ret_gn_b": nrm(ks[16], (DEPTH, RET_WIDTH), 0.02),
        "w_mem_k": nrm(ks[17], (DEPTH, D_MODEL, MEM_WIDTH), D_MODEL ** -0.5),
        "w_mem_v": nrm(ks[18], (DEPTH, D_MODEL, MEM_WIDTH), b * D_MODEL ** -0.5),
        "w_br_mla": nrm(ks[19], (DEPTH, MLA_WIDTH, D_MODEL), b * MLA_WIDTH ** -0.5),
        "w_br_ret": nrm(ks[20], (DEPTH, RET_WIDTH, D_MODEL), b * RET_WIDTH ** -0.5),
        "w_br_mem": nrm(ks[21], (DEPTH, MEM_WIDTH, D_MODEL), b * MEM_WIDTH ** -0.5),
        "w_o": nrm(ks[22], (DEPTH, D_MODEL, D_MODEL), b * D_MODEL ** -0.5),
        "ln1_g": 1.0 + nrm(ks[23], (DEPTH, D_MODEL), 0.02),
        "ln1_b": nrm(ks[24], (DEPTH, D_MODEL), 0.02),
        "w_up": nrm(ks[25], (DEPTH, D_MODEL, D_FF), D_MODEL ** -0.5),
        "w_down": nrm(ks[26], (DEPTH, D_FF, D_MODEL), b * D_FF ** -0.5),
        "ln2_g": 1.0 + nrm(ks[27], (DEPTH, D_MODEL), 0.02),
        "ln2_b": nrm(ks[28], (DEPTH, D_MODEL), 0.02),
    }


def reference(x_prompt, x_sample, mem_prompt, cache_mla_latent, cache_mla_rope, page_table,
              state_ret, cache_mem_k, cache_mem_v, w_in, q_norm_g, kv_norm_g, w_uq, w_uk, w_uv,
              ret_gn_g, ret_gn_b, w_mem_k, w_mem_v, w_br_mla, w_br_ret, w_br_mem, w_o,
              ln1_g, ln1_b, w_up, w_down, ln2_g, ln2_b):
    log_gamma = jnp.log(1.0 - jnp.power(2.0, -5.0 - jnp.arange(N_RET_HEADS, dtype=jnp.float32)))
    t_p = x_prompt.shape[1]
    b_s, t_s = x_sample.shape[:2]
    past = page_table.shape[1] * cache_mla_latent.shape[2]
    pos_p = jnp.arange(t_p)
    pos_s = past + jnp.arange(t_s)

    xp, xs = x_prompt, x_sample
    lat_p, rope_p, ret_p, memk_p, memv_p = [], [], [], [], []
    lat_s, rope_s, ret_s = [], [], []
    for l in range(DEPTH):
        ql, qr, c, kr, rq, rk, rv, rg, mq, gates = mixer_inputs(
            xp, pos_p, w_in[l], q_norm_g[l], kv_norm_g[l], w_uq[l], w_uk[l])
        o_lat = mla_prompt(ql, qr, c, kr)
        o_ret, S_p = retention_prompt(rq, rk, rv, log_gamma)
        mk = mem_prompt @ w_mem_k[l]
        mv = mem_prompt @ w_mem_v[l]
        o_mem = mem_attention(mq, mk, mv)
        a = merge_branches(gates, o_lat, o_ret, rg, o_mem, w_uv[l], ret_gn_g[l], ret_gn_b[l],
                           w_br_mla[l], w_br_ret[l], w_br_mem[l], w_o[l])
        xp = post_mixer(xp, a, ln1_g[l], ln1_b[l], w_up[l], w_down[l], ln2_g[l], ln2_b[l])
        lat_p.append(c)
        rope_p.append(kr)
        ret_p.append(S_p)
        memk_p.append(mk)
        memv_p.append(mv)

        ql, qr, c, kr, rq, rk, rv, rg, mq, gates = mixer_inputs(
            xs, pos_s, w_in[l], q_norm_g[l], kv_norm_g[l], w_uq[l], w_uk[l])
        c_past = cache_mla_latent[l, page_table].reshape(b_s, past, KV_LORA)
        kr_past = cache_mla_rope[l, page_table].reshape(b_s, past, QK_ROPE)
        o_lat = mla_sample(ql, qr, c, kr, c_past, kr_past)
        S_s, o_ret = retention_chunk(state_ret[l], rq.transpose(0, 2, 1, 3), rk.transpose(0, 2, 1, 3),
                                     rv.transpose(0, 2, 1, 3), log_gamma)
        o_ret = o_ret.transpose(0, 2, 1, 3)
        o_mem = mem_attention(mq, cache_mem_k[l], cache_mem_v[l])
        a = merge_branches(gates, o_lat, o_ret, rg, o_mem, w_uv[l], ret_gn_g[l], ret_gn_b[l],
                           w_br_mla[l], w_br_ret[l], w_br_mem[l], w_o[l])
        xs = post_mixer(xs, a, ln1_g[l], ln1_b[l], w_up[l], w_down[l], ln2_g[l], ln2_b[l])
        lat_s.append(c)
        rope_s.append(kr)
        ret_s.append(S_s)

    new_lat_prompt = jnp.stack(lat_p)
    new_rope_prompt = jnp.stack(rope_p)
    new_ret_prompt = jnp.stack(ret_p)
    new_memk_prompt = jnp.stack(memk_p)
    new_memv_prompt = jnp.stack(memv_p)
    new_lat_sample = jnp.stack(lat_s)
    new_rope_sample = jnp.stack(rope_s)
    new_ret_sample = jnp.stack(ret_s)
    return (xp, xs, new_lat_prompt, new_rope_prompt, new_ret_prompt, new_memk_prompt,
            new_memv_prompt, new_lat_sample, new_rope_sample, new_ret_sample)
```

```python
import functools

import jax
import jax.numpy as jnp
from jax import lax
from jax.experimental import pallas as pl
from jax.experimental.pallas import tpu as pltpu

F32 = jnp.float32
BF16 = jnp.bfloat16

N_RET_HEADS = 4
N_MEM_HEADS = 4
RET_CHUNK = 128
LN_EPS = 1e-5
RMS_EPS = 1e-6
ROPE_BASE = 10000.0
LANES = 128
VMEM_LIMIT = 52 * 1024 * 1024
MLA_Q_TILE = 256
DECODE_PAGES = 16


def _params(n_axes, vmem=VMEM_LIMIT):
    return pltpu.CompilerParams(dimension_semantics=("arbitrary",) * n_axes, vmem_limit_bytes=vmem)


def _tile(n, target, mult=16):
    best = None
    for t in range(mult, min(n, target) + 1, mult):
        if n % t == 0:
            best = t
    assert best is not None, (n, target, mult)
    return best


def _dot(a, b):
    return jnp.dot(a, b, preferred_element_type=F32)


def _dot_t(a, b):
    return lax.dot_general(a, b, (((1,), (1,)), ((), ())), preferred_element_type=F32)


def _div_pow2(x, d):
    assert d & (d - 1) == 0
    return lax.shift_right_logical(x, jnp.int32(d.bit_length() - 1))


def _sigmoid(x):
    return 1.0 / (1.0 + jnp.exp(-x))


def _rms_norm(x, g):
    return x * lax.rsqrt(jnp.mean(x * x, axis=-1, keepdims=True) + RMS_EPS) * g


def _layer_norm(u, g, b):
    mu = jnp.mean(u, axis=-1, keepdims=True)
    d = u - mu
    var = jnp.mean(d * d, axis=-1, keepdims=True)
    return d * lax.rsqrt(var + LN_EPS) * g + b


def _rope_padded(v, cos_t, sin_a, sin_b):
    q = LANES // 4
    return v * cos_t + pltpu.roll(v, LANES - q, 1) * sin_a + pltpu.roll(v, q, 1) * sin_b


def _rope_halves(x, cos, sin):
    h = x.shape[-1] // 2
    x1, x2 = x[:, :h], x[:, h:]
    return jnp.concatenate([x1 * cos - x2 * sin, x2 * cos + x1 * sin], axis=-1)


def _mm_kernel(x_ref, w_ref, o_ref, *, epilogue):
    acc = _dot(x_ref[...], w_ref[...])
    if epilogue == "relu_sq":
        acc = jnp.square(jnp.maximum(acc, 0.0))
    o_ref[...] = acc.astype(o_ref.dtype)


def _matmul(x, w, out_dtype, *, tm, tn, epilogue=None, name):
    m, k = x.shape
    n = w.shape[1]
    return pl.pallas_call(
        functools.partial(_mm_kernel, epilogue=epilogue),
        grid=(n // tn, m // tm),
        in_specs=[pl.BlockSpec((tm, k), lambda j, i: (i, 0)),
                  pl.BlockSpec((k, tn), lambda j, i: (0, j))],
        out_specs=pl.BlockSpec((tm, tn), lambda j, i: (i, j)),
        out_shape=jax.ShapeDtypeStruct((m, n), out_dtype),
        compiler_params=_params(2),
        name=name,
    )(x, w)


def _mla_prep_kernel(x_ref, w_ref, qg_ref, kvg_ref, wuq_ref, wuk_ref, cos_ref, sa_ref, sb_ref,
                     q_ref, kv_ref, lat_ref, rope_ref, *, n_heads, q_lora, kv_lora, nope, rope_d):
    zz = _dot(x_ref[...], w_ref[...])
    cos_t, sin_a, sin_b = cos_ref[...], sa_ref[...], sb_ref[...]
    c_q = _rms_norm(zz[:, :q_lora], qg_ref[...])
    q = _dot(c_q.astype(BF16), wuq_ref[...])
    c_kv = _rms_norm(zz[:, q_lora:q_lora + kv_lora], kvg_ref[...])
    lat_ref[...] = c_kv
    kr = _rope_padded(zz[:, q_lora + kv_lora:], cos_t, sin_a, sin_b)
    rope_ref[...] = kr[:, :rope_d]
    kv_ref[:, :kv_lora] = c_kv.astype(BF16)
    kv_ref[:, kv_lora:] = kr.astype(BF16)
    hw = nope + LANES
    for h in range(n_heads):
        q_nope = q[:, h * hw:h * hw + nope]
        q_rot = _rope_padded(q[:, h * hw + nope:(h + 1) * hw], cos_t, sin_a, sin_b)
        q_ref[h, :, :kv_lora] = _dot(q_nope.astype(BF16), wuk_ref[h]).astype(BF16)
        q_ref[h, :, kv_lora:] = q_rot.astype(BF16)


def _mla_prep(xb, w_mla, qg, kvg, wuq, wuk_t, tabs, *, tm, rope_d):
    n, d = xb.shape
    n_heads, nope, kv_lora = wuk_t.shape
    q_lora = wuq.shape[0]
    kw = kv_lora + LANES
    cos_t, sin_a, sin_b = tabs
    const = lambda shape: pl.BlockSpec(shape, lambda i: (0,) * len(shape))
    row = lambda w: pl.BlockSpec((tm, w), lambda i: (i, 0))
    return pl.pallas_call(
        functools.partial(_mla_prep_kernel, n_heads=n_heads, q_lora=q_lora, kv_lora=kv_lora,
                          nope=nope, rope_d=rope_d),
        grid=(n // tm,),
        in_specs=[row(d), const(w_mla.shape), const(qg.shape), const(kvg.shape), const(wuq.shape),
                  const(wuk_t.shape), row(LANES), row(LANES), row(LANES)],
        out_specs=[pl.BlockSpec((n_heads, tm, kw), lambda i: (0, i, 0)), row(kw), row(kv_lora), row(rope_d)],
        out_shape=[jax.ShapeDtypeStruct((n_heads, n, kw), BF16),
                   jax.ShapeDtypeStruct((n, kw), BF16),
                   jax.ShapeDtypeStruct((n, kv_lora), F32),
                   jax.ShapeDtypeStruct((n, rope_d), F32)],
        compiler_params=_params(1),
        name="mla_prep",
    )(xb, w_mla, qg, kvg, wuq, wuk_t, cos_t, sin_a, sin_b)


def _mla_prompt_kernel(q_ref, kv_ref, wuv_ref, o_ref, m_ref, l_ref, acc_ref, *, scale, kv_lora):
    n_heads, tq, kw = q_ref.shape
    rows = n_heads * tq
    qi = pl.program_id(1)
    q = q_ref[...].reshape(rows, kw)
    m_ref[...] = jnp.full(m_ref.shape, -jnp.inf, F32)
    l_ref[...] = jnp.zeros(l_ref.shape, F32)
    acc_ref[...] = jnp.zeros(acc_ref.shape, F32)

    def block(ki, masked):
        k = kv_ref[pl.ds(pl.multiple_of(ki * tq, tq), tq), :]
        s = _dot_t(q, k) * scale
        if masked:
            q_pos = lax.broadcasted_iota(jnp.int32, s.shape, 0) & (tq - 1)
            k_pos = lax.broadcasted_iota(jnp.int32, s.shape, 1)
            s = jnp.where(k_pos <= q_pos, s, -jnp.inf)
        m_prev = m_ref[...]
        m_new = jnp.maximum(m_prev, jnp.max(s, axis=-1, keepdims=True))
        alpha = jnp.exp(m_prev - m_new)
        p = jnp.exp(s - m_new)
        l_ref[...] = alpha * l_ref[...] + jnp.sum(p, axis=-1, keepdims=True)
        acc_ref[...] = alpha * acc_ref[...] + _dot(p.astype(BF16), k[:, :kv_lora])
        m_ref[...] = m_new

    def body(ki, carry):
        block(ki, False)
        return carry

    lax.fori_loop(0, qi, body, 0)
    block(qi, True)
    o = (acc_ref[...] / l_ref[...]).astype(BF16)
    vh = wuv_ref.shape[2]
    for h in range(n_heads):
        o_ref[:, h * vh:(h + 1) * vh] = _dot(o[h * tq:(h + 1) * tq], wuv_ref[h]).astype(o_ref.dtype)


def _mla_prompt(q_all, kv_all, wuv_t, *, batch, seq, scale):
    n_heads, _, kw = q_all.shape
    _, kv_lora, vh = wuv_t.shape
    tq = MLA_Q_TILE
    nq = seq // tq
    return pl.pallas_call(
        functools.partial(_mla_prompt_kernel, scale=scale, kv_lora=kv_lora),
        grid=(batch, nq),
        in_specs=[pl.BlockSpec((n_heads, tq, kw), lambda b, i: (0, b * nq + i, 0)),
                  pl.BlockSpec((seq, kw), lambda b, i: (b, 0)),
                  pl.BlockSpec(wuv_t.shape, lambda b, i: (0, 0, 0))],
        out_specs=pl.BlockSpec((tq, n_heads * vh), lambda b, i: (b * nq + i, 0)),
        out_shape=jax.ShapeDtypeStruct((batch * seq, n_heads * vh), BF16),
        scratch_shapes=[pltpu.VMEM((n_heads * tq, 1), F32), pltpu.VMEM((n_heads * tq, 1), F32),
                        pltpu.VMEM((n_heads * tq, kv_lora), F32)],
        compiler_params=_params(2),
        name="mla_prompt",
    )(q_all, kv_all, wuv_t)


def _mla_decode_kernel(pt_ref, q_ref, kvn_ref, lat_hbm, rope_hbm, o_ref,
                       lat_buf, rope_buf, sems, m_ref, l_ref, acc_ref,
                       *, layer, n_chunks, scale, kv_lora, rope_d, dec_seq):
    g = pl.program_id(0)
    n_steps = pl.num_programs(0)
    page = lat_hbm.shape[2]

    def copies(step, slot):
        out = []
        for p in range(DECODE_PAGES):
            idx = pt_ref[step * DECODE_PAGES + p]
            rows = pl.ds(p * page, page)
            out.append(pltpu.make_async_copy(lat_hbm.at[layer, idx], lat_buf.at[slot, rows], sems.at[slot, 0]))
            out.append(pltpu.make_async_copy(rope_hbm.at[layer, idx], rope_buf.at[slot, rows], sems.at[slot, 1]))
        return out

    @pl.when(g == 0)
    def _():
        for c in copies(0, 0):
            c.start()

    @pl.when(g + 1 < n_steps)
    def _():
        for c in copies(g + 1, (g + 1) % 2):
            c.start()

    slot = g % 2
    for c in copies(g, slot):
        c.wait()

    c_idx = g % n_chunks

    @pl.when(c_idx == 0)
    def _():
        m_ref[...] = jnp.full(m_ref.shape, -jnp.inf, F32)
        l_ref[...] = jnp.zeros(l_ref.shape, F32)
        acc_ref[...] = jnp.zeros(acc_ref.shape, F32)

    q = q_ref[0]

    def accumulate(s, v):
        m_prev = m_ref[...]
        m_new = jnp.maximum(m_prev, jnp.max(s, axis=-1, keepdims=True))
        alpha = jnp.exp(m_prev - m_new)
        p = jnp.exp(s - m_new)
        l_ref[...] = alpha * l_ref[...] + jnp.sum(p, axis=-1, keepdims=True)
        acc_ref[...] = alpha * acc_ref[...] + _dot(p.astype(BF16), v)
        m_ref[...] = m_new

    kc = lat_buf[slot].astype(BF16)
    kr = rope_buf[slot].astype(BF16)
    s_past = (_dot_t(q[:, :kv_lora], kc) + _dot_t(q[:, kv_lora:kv_lora + rope_d], kr)) * scale
    accumulate(s_past, kc)

    @pl.when(c_idx == n_chunks - 1)
    def _():
        kn = kvn_ref[0]
        s_new = _dot_t(q, kn) * scale
        t_q = lax.broadcasted_iota(jnp.int32, s_new.shape, 0) & (dec_seq - 1)
        t_k = lax.broadcasted_iota(jnp.int32, s_new.shape, 1)
        s_new = jnp.where(t_k <= t_q, s_new, -jnp.inf)
        accumulate(s_new, kn[:, :kv_lora])
        o_ref[0] = (acc_ref[...] / l_ref[...]).astype(o_ref.dtype)


def _mla_decode(pt_flat, q_s, kvn_s, cache_lat, cache_rope, *, layer, scale, dec_seq):
    bs, rows, kw = q_s.shape
    kv_lora = cache_lat.shape[-1]
    rope_d = cache_rope.shape[-1]
    page = cache_lat.shape[2]
    n_pages = pt_flat.shape[0] // bs
    assert n_pages % DECODE_PAGES == 0
    n_chunks = n_pages // DECODE_PAGES
    grid_spec = pltpu.PrefetchScalarGridSpec(
        num_scalar_prefetch=1,
        grid=(bs * n_chunks,),
        in_specs=[pl.BlockSpec((1, rows, kw), lambda g, pt: (g // n_chunks, 0, 0)),
                  pl.BlockSpec((1,) + kvn_s.shape[1:], lambda g, pt: (g // n_chunks, 0, 0)),
                  pl.BlockSpec(memory_space=pl.ANY),
                  pl.BlockSpec(memory_space=pl.ANY)],
        out_specs=pl.BlockSpec((1, rows, kv_lora), lambda g, pt: (g // n_chunks, 0, 0)),
        scratch_shapes=[pltpu.VMEM((2, DECODE_PAGES * page, kv_lora), F32),
                        pltpu.VMEM((2, DECODE_PAGES * page, rope_d), F32),
                        pltpu.SemaphoreType.DMA((2, 2)),
                        pltpu.VMEM((rows, 1), F32), pltpu.VMEM((rows, 1), F32),
                        pltpu.VMEM((rows, kv_lora), F32)],
    )
    return pl.pallas_call(
        functools.partial(_mla_decode_kernel, layer=layer, n_chunks=n_chunks, scale=scale,
                          kv_lora=kv_lora, rope_d=rope_d, dec_seq=dec_seq),
        grid_spec=grid_spec,
        out_shape=jax.ShapeDtypeStruct((bs, rows, kv_lora), BF16),
        compiler_params=_params(1),
        name="mla_decode",
    )(pt_flat, q_s, kvn_s, cache_lat, cache_rope)


def _uv_proj_kernel(o_ref, w_ref, y_ref):
    vh = w_ref.shape[2]
    for h in range(o_ref.shape[0]):
        y_ref[:, h * vh:(h + 1) * vh] = _dot(o_ref[h], w_ref[h]).astype(y_ref.dtype)


def _uv_proj(o_heads, wuv_t):
    n_heads, rows, _ = o_heads.shape
    vh = wuv_t.shape[2]
    return pl.pallas_call(
        _uv_proj_kernel,
        out_shape=jax.ShapeDtypeStruct((rows, n_heads * vh), BF16),
        compiler_params=pltpu.CompilerParams(vmem_limit_bytes=VMEM_LIMIT),
        name="uv_proj",
    )(o_heads, wuv_t)


def _ret_tail(o, g_raw, gn_g, gn_b):
    mu = jnp.mean(o, axis=-1, keepdims=True)
    d = o - mu
    var = jnp.mean(d * d, axis=-1, keepdims=True)
    gn = d * lax.rsqrt(var + LN_EPS) * gn_g + gn_b
    return g_raw * _sigmoid(g_raw) * gn


def _ret_prompt_kernel(q_ref, k_ref, v_ref, g_ref, cos_ref, sin_ref, intra_ref, qd_ref, kd_ref, cd_ref,
                       gng_ref, gnb_ref, y_ref, s_ref, *, k_scale):
    c = pl.program_id(1)
    hd = s_ref.shape[-1]

    @pl.when(c == 0)
    def _():
        s_ref[...] = jnp.zeros(s_ref.shape, F32)

    cos, sin = cos_ref[...], sin_ref[...]
    for h in range(N_RET_HEADS):
        cols = slice(h * hd, (h + 1) * hd)
        rq = _rope_halves(q_ref[:, cols], cos, sin)
        rk = _rope_halves(k_ref[:, cols], cos, sin) * k_scale
        qb, kb, vb = rq.astype(BF16), rk.astype(BF16), v_ref[:, cols].astype(BF16)
        state = s_ref[0, h]
        scores = _dot_t(qb, kb) * intra_ref[h]
        o = _dot(scores.astype(BF16), vb) + _dot(qb, state.astype(BF16)) * qd_ref[h]
        kd_t = (rk * kd_ref[h]).T.astype(BF16)
        s_ref[0, h] = state * cd_ref[h] + _dot(kd_t, vb)
        y_ref[:, cols] = _ret_tail(o, g_ref[:, cols], gng_ref[:, cols], gnb_ref[:, cols]).astype(y_ref.dtype)


def _ret_prompt(z, cos_r, sin_r, dec, gn_g, gn_b, *, batch, seq, k_scale):
    width = gn_g.shape[1]
    hd = width // N_RET_HEADS
    nc = seq // RET_CHUNK
    intra, qd, kd, cd = dec
    zcol = lambda j: pl.BlockSpec((RET_CHUNK, width), lambda b, c, j=j: (b * nc + c, j))
    const = lambda a: pl.BlockSpec(a.shape, lambda b, c: (0,) * a.ndim)
    return pl.pallas_call(
        functools.partial(_ret_prompt_kernel, k_scale=k_scale),
        grid=(batch, nc),
        in_specs=[zcol(0), zcol(1), zcol(2), zcol(3),
                  pl.BlockSpec((RET_CHUNK, hd // 2), lambda b, c: (b * nc + c, 0)),
                  pl.BlockSpec((RET_CHUNK, hd // 2), lambda b, c: (b * nc + c, 0)),
                  const(intra), const(qd), const(kd), const(cd), const(gn_g), const(gn_b)],
        out_specs=[pl.BlockSpec((RET_CHUNK, width), lambda b, c: (b * nc + c, 0)),
                   pl.BlockSpec((1, N_RET_HEADS, hd, hd), lambda b, c: (b, 0, 0, 0))],
        out_shape=[jax.ShapeDtypeStruct((batch * seq, width), BF16),
                   jax.ShapeDtypeStruct((batch, N_RET_HEADS, hd, hd), F32)],
        compiler_params=_params(2),
        name="ret_prompt",
    )(z, z, z, z, cos_r, sin_r, intra, qd, kd, cd, gn_g, gn_b)


def _ret_sample_kernel(q_ref, k_ref, v_ref, g_ref, cos_ref, sin_ref, st_ref, intra_ref, qd_ref, kd_ref, cd_ref,
                       gng_ref, gnb_ref, y_ref, so_ref, *, k_scale, dec_seq):
    bt = so_ref.shape[0]
    hd = so_ref.shape[-1]
    rows = q_ref.shape[0]
    cos, sin = cos_ref[...], sin_ref[...]
    row_b = _div_pow2(lax.broadcasted_iota(jnp.int32, (rows, hd), 0), dec_seq)
    for h in range(N_RET_HEADS):
        cols = slice(h * hd, (h + 1) * hd)
        rq = _rope_halves(q_ref[:, cols], cos, sin)
        rk = _rope_halves(k_ref[:, cols], cos, sin) * k_scale
        qb, kb, vb = rq.astype(BF16), rk.astype(BF16), v_ref[:, cols].astype(BF16)
        scores = _dot_t(qb, kb) * intra_ref[h]
        o = _dot(scores.astype(BF16), vb)
        kd = rk * kd_ref[h]
        for b in range(bt):
            state = st_ref[0, b, h]
            mine = row_b == b
            o = o + jnp.where(mine, _dot(qb, state.astype(BF16)) * qd_ref[h], 0.0)
            kd_t = jnp.where(mine, kd, 0.0).T.astype(BF16)
            so_ref[b, h] = state * cd_ref[h] + _dot(kd_t, vb)
        y_ref[:, cols] = _ret_tail(o, g_ref[:, cols], gng_ref[:, cols], gnb_ref[:, cols]).astype(y_ref.dtype)


def _ret_sample(z, cos_r, sin_r, state_all, dec, gn_g, gn_b, *, layer, row0, bs, dec_seq, bt, k_scale):
    width = gn_g.shape[1]
    hd = width // N_RET_HEADS
    rows = bt * dec_seq
    r0 = row0 // rows
    intra, qd, kd, cd = dec
    zcol = lambda j: pl.BlockSpec((rows, width), lambda i, j=j: (r0 + i, j))
    const = lambda a: pl.BlockSpec(a.shape, lambda i: (0,) * a.ndim)
    return pl.pallas_call(
        functools.partial(_ret_sample_kernel, k_scale=k_scale, dec_seq=dec_seq),
        grid=(bs // bt,),
        in_specs=[zcol(0), zcol(1), zcol(2), zcol(3),
                  pl.BlockSpec((rows, hd // 2), lambda i: (r0 + i, 0)),
                  pl.BlockSpec((rows, hd // 2), lambda i: (r0 + i, 0)),
                  pl.BlockSpec((1, bt, N_RET_HEADS, hd, hd), lambda i: (layer, i, 0, 0, 0)),
                  const(intra), const(qd), const(kd), const(cd), const(gn_g), const(gn_b)],
        out_specs=[pl.BlockSpec((rows, width), lambda i: (i, 0)),
                   pl.BlockSpec((bt, N_RET_HEADS, hd, hd), lambda i: (i, 0, 0, 0))],
        out_shape=[jax.ShapeDtypeStruct((bs * dec_seq, width), BF16),
                   jax.ShapeDtypeStruct((bs, N_RET_HEADS, hd, hd), F32)],
        compiler_params=_params(1),
        name="ret_sample",
    )(z, z, z, z, cos_r, sin_r, state_all, intra, qd, kd, cd, gn_g, gn_b)


def _softmax_rows(s):
    e = jnp.exp(s - jnp.max(s, axis=-1, keepdims=True))
    return e / jnp.sum(e, axis=-1, keepdims=True)


def _mem_prompt_kernel(q_ref, kv_ref, o_ref, *, scale):
    width = o_ref.shape[1]
    hd = width // N_MEM_HEADS
    for h in range(N_MEM_HEADS):
        cols = slice(h * hd, (h + 1) * hd)
        k = kv_ref[:, cols].astype(BF16)
        v = kv_ref[:, width + h * hd:width + (h + 1) * hd].astype(BF16)
        p = _softmax_rows(_dot_t(q_ref[:, cols].astype(BF16), k) * scale)
        o_ref[:, cols] = _dot(p.astype(BF16), v).astype(o_ref.dtype)


def _mem_prompt(z, mkv, *, batch, seq, n_mem, width, qcol, tm, scale):
    nt = seq // tm
    return pl.pallas_call(
        functools.partial(_mem_prompt_kernel, scale=scale),
        grid=(batch, nt),
        in_specs=[pl.BlockSpec((tm, width), lambda b, i: (b * nt + i, qcol)),
                  pl.BlockSpec((n_mem, 2 * width), lambda b, i: (b, 0))],
        out_specs=pl.BlockSpec((tm, width), lambda b, i: (b * nt + i, 0)),
        out_shape=jax.ShapeDtypeStruct((batch * seq, width), BF16),
        compiler_params=_params(2),
        name="mem_prompt",
    )(z, mkv)


def _mem_sample_kernel(q_ref, k_ref, v_ref, o_ref, *, scale, dec_seq):
    bt = k_ref.shape[1]
    width = o_ref.shape[1]
    hd = width // N_MEM_HEADS
    rows = q_ref.shape[0]
    row_b = _div_pow2(lax.broadcasted_iota(jnp.int32, (rows, hd), 0), dec_seq)
    for h in range(N_MEM_HEADS):
        cols = slice(h * hd, (h + 1) * hd)
        qb = q_ref[:, cols].astype(BF16)
        o = jnp.zeros((rows, hd), F32)
        for b in range(bt):
            p = _softmax_rows(_dot_t(qb, k_ref[0, b, :, cols].astype(BF16)) * scale)
            o = jnp.where(row_b == b, _dot(p.astype(BF16), v_ref[0, b, :, cols].astype(BF16)), o)
        o_ref[:, cols] = o.astype(o_ref.dtype)


def _mem_sample(z, ck, cv, *, layer, row0, bs, dec_seq, bt, width, qcol, scale):
    rows = bt * dec_seq
    r0 = row0 // rows
    n_mem = ck.shape[2]
    cache = pl.BlockSpec((1, bt, n_mem, width), lambda i: (layer, i, 0, 0))
    return pl.pallas_call(
        functools.partial(_mem_sample_kernel, scale=scale, dec_seq=dec_seq),
        grid=(bs // bt,),
        in_specs=[pl.BlockSpec((rows, width), lambda i: (r0 + i, qcol)), cache, cache],
        out_specs=pl.BlockSpec((rows, width), lambda i: (i, 0)),
        out_shape=jax.ShapeDtypeStruct((bs * dec_seq, width), BF16),
        compiler_params=_params(1),
        name="mem_sample",
    )(z, ck, cv)


def _merge_kernel(ya_ref, yb_ref, yc_ref, ga_ref, gb_ref, gc_ref, wa_ref, wb_ref, wc_ref, o_ref):
    acc = _sigmoid(ga_ref[...]) * _dot(ya_ref[...], wa_ref[...])
    acc = acc + _sigmoid(gb_ref[...]) * _dot(yb_ref[...], wb_ref[...])
    acc = acc + _sigmoid(gc_ref[...]) * _dot(yc_ref[...], wc_ref[...])
    o_ref[...] = acc.astype(o_ref.dtype)


def _merge(ya, yb, yc, z, wa, wb, wc, *, gate_col0, tm, tn):
    n, width = ya.shape
    d = wa.shape[1]
    nj = d // tn
    g0 = gate_col0 // tn
    y = pl.BlockSpec((tm, width), lambda j, i: (i, 0))
    w = pl.BlockSpec((width, tn), lambda j, i: (0, j))
    gate = lambda k: pl.BlockSpec((tm, tn), lambda j, i, k=k: (i, g0 + k * nj + j))
    return pl.pallas_call(
        _merge_kernel,
        grid=(nj, n // tm),
        in_specs=[y, y, y, gate(0), gate(1), gate(2), w, w, w],
        out_specs=pl.BlockSpec((tm, tn), lambda j, i: (i, j)),
        out_shape=jax.ShapeDtypeStruct((n, d), BF16),
        compiler_params=_params(2),
        name="merge",
    )(ya, yb, yc, z, z, z, wa, wb, wc)


def _oproj_ln_kernel(m_ref, w_ref, x_ref, g_ref, b_ref, y_ref, yb_ref, *, alpha):
    y = _layer_norm(alpha * x_ref[...] + _dot(m_ref[...], w_ref[...]), g_ref[...], b_ref[...])
    y_ref[...] = y
    yb_ref[...] = y.astype(BF16)


def _oproj_ln(merged, w_o, x, g, b, *, alpha, tm):
    n, d = x.shape
    row = pl.BlockSpec((tm, d), lambda i: (i, 0))
    const = lambda a: pl.BlockSpec(a.shape, lambda i: (0,) * a.ndim)
    return pl.pallas_call(
        functools.partial(_oproj_ln_kernel, alpha=alpha),
        grid=(n // tm,),
        in_specs=[row, const(w_o), row, const(g), const(b)],
        out_specs=[row, row],
        out_shape=[jax.ShapeDtypeStruct((n, d), F32), jax.ShapeDtypeStruct((n, d), BF16)],
        compiler_params=_params(1),
        name="oproj_ln",
    )(merged, w_o, x, g, b)


def _ffn_down_ln_kernel(h_ref, w_ref, x_ref, g_ref, b_ref, y_ref, yb_ref, acc_ref, *, alpha):
    k = pl.program_id(1)
    part = _dot(h_ref[...], w_ref[...])

    @pl.when(k == 0)
    def _():
        acc_ref[...] = part

    @pl.when(k > 0)
    def _():
        acc_ref[...] += part

    @pl.when(k == pl.num_programs(1) - 1)
    def _():
        y = _layer_norm(alpha * x_ref[...] + acc_ref[...], g_ref[...], b_ref[...])
        y_ref[...] = y
        yb_ref[...] = y.astype(BF16)


def _ffn_down_ln(h, w_down, x, g, b, *, alpha, tm, tk):
    n, d = x.shape
    ff = h.shape[1]
    row = pl.BlockSpec((tm, d), lambda i, k: (i, 0))
    const = lambda a: pl.BlockSpec(a.shape, lambda i, k: (0,) * a.ndim)
    return pl.pallas_call(
        functools.partial(_ffn_down_ln_kernel, alpha=alpha),
        grid=(n // tm, ff // tk),
        in_specs=[pl.BlockSpec((tm, tk), lambda i, k: (i, k)),
                  pl.BlockSpec((tk, d), lambda i, k: (k, 0)),
                  row, const(g), const(b)],
        out_specs=[row, row],
        out_shape=[jax.ShapeDtypeStruct((n, d), F32), jax.ShapeDtypeStruct((n, d), BF16)],
        scratch_shapes=[pltpu.VMEM((tm, d), F32)],
        compiler_params=_params(2),
        name="ffn_down_ln",
    )(h, w_down, x, g, b)


def _rope_tables(pos, rope_d, ret_hd):
    posf = pos.astype(F32)[:, None]

    def cos_sin(d):
        half = d // 2
        inv = jnp.power(ROPE_BASE, -jnp.arange(half, dtype=F32) * (2.0 / d))
        ang = posf * inv[None, :]
        return jnp.cos(ang), jnp.sin(ang)

    cos, sin = cos_sin(rope_d)
    zero = jnp.zeros_like(cos)
    pad = jnp.zeros((cos.shape[0], LANES - 2 * cos.shape[1]), F32)
    mla = (jnp.concatenate([cos, cos, pad], 1),
           jnp.concatenate([-sin, zero, pad], 1),
           jnp.concatenate([zero, sin, pad], 1))
    return mla, cos_sin(ret_hd)


def _decay_tables(log_gamma, chunk, groups, hd):
    idx = jnp.arange(chunk, dtype=F32)
    lg = log_gamma[:, None]
    diff = idx[:, None] - idx[None, :]
    intra = jnp.where(diff >= 0, jnp.exp(lg[:, :, None] * jnp.maximum(diff, 0.0)), 0.0)
    q_decay = jnp.exp(lg * (idx + 1.0))
    k_decay = jnp.exp(lg * (chunk - 1.0 - idx))
    chunk_decay = jnp.exp(log_gamma * chunk)
    nh = log_gamma.shape[0]
    eye = jnp.eye(groups, dtype=F32)
    intra = (eye[None, :, None, :, None] * intra[:, None, :, None, :]).reshape(nh, groups * chunk, groups * chunk)
    q_decay = jnp.broadcast_to(jnp.tile(q_decay, (1, groups))[:, :, None], (nh, groups * chunk, hd))
    k_decay = jnp.broadcast_to(jnp.tile(k_decay, (1, groups))[:, :, None], (nh, groups * chunk, hd))
    chunk_decay = jnp.broadcast_to(chunk_decay[:, None, None], (nh, 1, hd))
    return intra, q_decay, k_decay, chunk_decay


def kernel(x_prompt, x_sample, mem_prompt, cache_mla_latent, cache_mla_rope, page_table, state_ret, cache_mem_k, cache_mem_v, w_in, q_norm_g, kv_norm_g, w_uq, w_uk, w_uv, ret_gn_g, ret_gn_b, w_mem_k, w_mem_v, w_br_mla, w_br_ret, w_br_mem, w_o, ln1_g, ln1_b, w_up, w_down, ln2_g, ln2_b):
    depth = w_in.shape[0]
    batch, seq, d_model = x_prompt.shape
    bs, dec_seq, _ = x_sample.shape
    n_mem = mem_prompt.shape[1]
    kv_lora, n_heads, nope = w_uk.shape[1:]
    q_lora = w_uq.shape[1]
    rope_d = cache_mla_rope.shape[-1]
    ret_width = ret_gn_g.shape[1]
    ret_hd = ret_width // N_RET_HEADS
    mem_width = w_mem_k.shape[2]
    d_ff = w_up.shape[2]
    page = cache_mla_latent.shape[2]
    past = page_table.shape[1] * page
    n_p = batch * seq
    n_s = bs * dec_seq
    n = n_p + n_s
    alpha = (2 * depth) ** 0.25
    mla_scale = (nope + rope_d) ** -0.5
    assert 2 * rope_d <= LANES and (rope_d // 2) * 4 == LANES and ret_hd // 2 == LANES
    off_kv_end = q_lora + kv_lora + rope_d

    memq_col = 4 * ret_width
    gate_col = memq_col + mem_width

    pos = jnp.concatenate([jnp.arange(seq, dtype=jnp.int32)] * batch
                          + [past + jnp.arange(dec_seq, dtype=jnp.int32)] * bs)
    mla_tabs, (cos_r, sin_r) = _rope_tables(pos, rope_d, ret_hd)
    log_gamma = jnp.log(1.0 - jnp.power(2.0, -5.0 - jnp.arange(N_RET_HEADS, dtype=F32)))
    bt = 16 // dec_seq
    dec_p = _decay_tables(log_gamma, RET_CHUNK, 1, ret_hd)
    dec_s = _decay_tables(log_gamma, dec_seq, bt, ret_hd)

    tm_big = _tile(n, 1088)
    tm_mid = _tile(n, 544)
    tm_small = _tile(n, 256)
    tn = 512 if d_model % 512 == 0 else _tile(d_model, 512, LANES)

    x = jnp.concatenate([x_prompt.reshape(n_p, d_model), x_sample.reshape(n_s, d_model)], axis=0)
    xb = x.astype(BF16)
    mem_b = mem_prompt.reshape(batch * n_mem, d_model).astype(BF16)
    pt_flat = page_table.reshape(-1)

    outs = [[] for _ in range(8)]
    for l in range(depth):
        w_main = w_in[l, :, off_kv_end:].astype(BF16)
        w_mla = jnp.pad(w_in[l, :, :off_kv_end], ((0, 0), (0, LANES - rope_d))).astype(BF16)
        wq = w_uq[l].reshape(q_lora, n_heads, nope + rope_d)
        wq = jnp.pad(wq, ((0, 0), (0, 0), (0, LANES - rope_d))).reshape(q_lora, n_heads * (nope + LANES)).astype(BF16)
        wuk_t = jnp.transpose(w_uk[l], (1, 2, 0)).astype(BF16)
        wuv_t = jnp.transpose(w_uv[l], (1, 0, 2)).astype(BF16)
        w_mem = jnp.concatenate([w_mem_k[l], w_mem_v[l]], axis=1).astype(BF16)

        z = _matmul(xb, w_main, F32, tm=tm_big, tn=tn, name="in_proj")
        q_all, kv_all, lat, kro = _mla_prep(xb, w_mla, q_norm_g[l][None], kv_norm_g[l][None], wq, wuk_t, mla_tabs,
                                            tm=tm_mid, rope_d=rope_d)

        y_mla_p = _mla_prompt(q_all, kv_all, wuv_t, batch=batch, seq=seq, scale=mla_scale)
        q_s = q_all[:, n_p:].reshape(n_heads, bs, dec_seq, -1).transpose(1, 0, 2, 3).reshape(bs, n_heads * dec_seq, -1)
        kvn_s = jnp.pad(kv_all[n_p:].reshape(bs, dec_seq, -1), ((0, 0), (0, 16 - dec_seq), (0, 0)))
        o_lat_s = _mla_decode(pt_flat, q_s, kvn_s, cache_mla_latent, cache_mla_rope, layer=l, scale=mla_scale,
                              dec_seq=dec_seq)
        o_heads = o_lat_s.reshape(bs, n_heads, dec_seq, kv_lora).transpose(1, 0, 2, 3).reshape(n_heads, n_s, kv_lora)
        y_mla = jnp.concatenate([y_mla_p, _uv_proj(o_heads, wuv_t)], axis=0)

        gn_g, gn_b = ret_gn_g[l][None], ret_gn_b[l][None]
        y_ret_p, s_p = _ret_prompt(z, cos_r, sin_r, dec_p, gn_g, gn_b, batch=batch, seq=seq, k_scale=ret_hd ** -0.5)
        y_ret_s, s_s = _ret_sample(z, cos_r, sin_r, state_ret, dec_s, gn_g, gn_b, layer=l, row0=n_p, bs=bs,
                                   dec_seq=dec_seq, bt=bt, k_scale=ret_hd ** -0.5)
        y_ret = jnp.concatenate([y_ret_p, y_ret_s], axis=0)

        mkv = _matmul(mem_b, w_mem, F32, tm=_tile(batch * n_mem, 512), tn=tn, name="mem_kv")
        mem_scale = (mem_width // N_MEM_HEADS) ** -0.5
        o_mem_p = _mem_prompt(z, mkv, batch=batch, seq=seq, n_mem=n_mem, width=mem_width,
                              qcol=memq_col // mem_width, tm=_tile(seq, 512), scale=mem_scale)
        o_mem_s = _mem_sample(z, cache_mem_k, cache_mem_v, layer=l, row0=n_p, bs=bs, dec_seq=dec_seq, bt=bt,
                              width=mem_width, qcol=memq_col // mem_width, scale=mem_scale)
        o_mem = jnp.concatenate([o_mem_p, o_mem_s], axis=0)

        merged = _merge(y_mla, y_ret, o_mem, z, w_br_mla[l].astype(BF16), w_br_ret[l].astype(BF16),
                        w_br_mem[l].astype(BF16), gate_col0=gate_col, tm=tm_mid, tn=tn)
        x1, x1b = _oproj_ln(merged, w_o[l].astype(BF16), x, ln1_g[l][None], ln1_b[l][None], alpha=alpha, tm=tm_small)
        hmid = _matmul(x1b, w_up[l].astype(BF16), BF16, tm=tm_big, tn=tn, epilogue="relu_sq", name="ffn_up")
        x, xb = _ffn_down_ln(hmid, w_down[l].astype(BF16), x1, ln2_g[l][None], ln2_b[l][None], alpha=alpha,
                             tm=tm_mid, tk=_tile(d_ff, 512, LANES))

        outs[0].append(lat[:n_p].reshape(batch, seq, kv_lora))
        outs[1].append(kro[:n_p].reshape(batch, seq, rope_d))
        outs[2].append(s_p)
        outs[3].append(mkv[:, :mem_width].reshape(batch, n_mem, mem_width))
        outs[4].append(mkv[:, mem_width:].reshape(batch, n_mem, mem_width))
        outs[5].append(lat[n_p:].reshape(bs, dec_seq, kv_lora))
        outs[6].append(kro[n_p:].reshape(bs, dec_seq, rope_d))
        outs[7].append(s_s)

    return (x[:n_p].reshape(batch, seq, d_model), x[n_p:].reshape(bs, dec_seq, d_model),
            *[jnp.stack(o) for o in outs])
```

```python
import functools

import jax
import jax.numpy as jnp
from jax import lax
from jax.experimental import pallas as pl
from jax.experimental.pallas import tpu as pltpu

F32 = jnp.float32
BF16 = jnp.bfloat16

N_RET_HEADS = 4
N_MEM_HEADS = 4
RET_CHUNK = 128
LN_EPS = 1e-5
RMS_EPS = 1e-6
ROPE_BASE = 10000.0
LANES = 128
VMEM_LIMIT = 52 * 1024 * 1024
MLA_Q_TILE = 256
DECODE_PAGES = 16


def _params(n_axes, vmem=VMEM_LIMIT):
    return pltpu.CompilerParams(dimension_semantics=("arbitrary",) * n_axes, vmem_limit_bytes=vmem)


def _tile(n, target, mult=16):
    best = None
    for t in range(mult, min(n, target) + 1, mult):
        if n % t == 0:
            best = t
    assert best is not None, (n, target, mult)
    return best


def _dot(a, b):
    return jnp.dot(a, b, preferred_element_type=F32)


def _dot_t(a, b):
    return lax.dot_general(a, b, (((1,), (1,)), ((), ())), preferred_element_type=F32)


def _div_pow2(x, d):
    assert d & (d - 1) == 0
    return lax.shift_right_logical(x, jnp.int32(d.bit_length() - 1))


def _lane_rep(col):
    return jnp.broadcast_to(col, (col.shape[0], LANES))


def _lane_tile(a, width):
    return jnp.concatenate([a] * (width // LANES), axis=1)


def _sigmoid(x):
    return 1.0 / (1.0 + jnp.exp(-x))


def _rms_norm(x, g):
    return x * lax.rsqrt(jnp.mean(x * x, axis=-1, keepdims=True) + RMS_EPS) * g


def _layer_norm(u, g, b):
    mu = jnp.mean(u, axis=-1, keepdims=True)
    d = u - mu
    var = jnp.mean(d * d, axis=-1, keepdims=True)
    return d * lax.rsqrt(var + LN_EPS) * g + b


def _rope_padded(v, cos_t, sin_a, sin_b):
    q = LANES // 4
    return v * cos_t + pltpu.roll(v, LANES - q, 1) * sin_a + pltpu.roll(v, q, 1) * sin_b


def _rope_halves(x, cos, sin):
    h = x.shape[-1] // 2
    x1, x2 = x[:, :h], x[:, h:]
    return jnp.concatenate([x1 * cos - x2 * sin, x2 * cos + x1 * sin], axis=-1)


def _mm_kernel(x_ref, w_ref, o_ref, *, epilogue):
    acc = _dot(x_ref[...], w_ref[...])
    if epilogue == "relu_sq":
        acc = jnp.square(jnp.maximum(acc, 0.0))
    o_ref[...] = acc.astype(o_ref.dtype)


def _layer_block(w_all, layer, block, index):
    return pl.BlockSpec((None,) + block, lambda *g: (layer,) + index(*g))


def _matmul(x, w_all, out_dtype, *, layer, tm, tn, epilogue=None, name):
    m, k = x.shape
    n = w_all.shape[2]
    return pl.pallas_call(
        functools.partial(_mm_kernel, epilogue=epilogue),
        grid=(n // tn, m // tm),
        in_specs=[pl.BlockSpec((tm, k), lambda j, i: (i, 0)),
                  _layer_block(w_all, layer, (k, tn), lambda j, i: (0, j))],
        out_specs=pl.BlockSpec((tm, tn), lambda j, i: (i, j)),
        out_shape=jax.ShapeDtypeStruct((m, n), out_dtype),
        compiler_params=_params(2),
        name=name,
    )(x, w_all)


def _rows_of(wt_all, layer, row0, rows, index_args):
    k = wt_all.shape[2]
    if index_args == 2:
        imap = lambda j, i: (layer, pl.multiple_of(row0 + j * rows, 8), 0)
    else:
        imap = lambda i: (layer, row0, 0)
    return pl.BlockSpec((pl.Element(1), pl.Element(rows), pl.Element(k)), imap)


def _mm_wt_kernel(x_ref, wt_ref, o_ref, wb_ref):
    @pl.when(pl.program_id(1) == 0)
    def _():
        wb_ref[...] = wt_ref[0].astype(BF16)

    o_ref[...] = _dot_t(x_ref[...], wb_ref[...]).astype(o_ref.dtype)


def _matmul_wt(x, wt_all, out_dtype, *, layer, row0, n_out, tm, tn, name):
    m, k = x.shape
    assert row0 % 8 == 0 and n_out % tn == 0
    return pl.pallas_call(
        _mm_wt_kernel,
        grid=(n_out // tn, m // tm),
        in_specs=[pl.BlockSpec((tm, k), lambda j, i: (i, 0)), _rows_of(wt_all, layer, row0, tn, 2)],
        out_specs=pl.BlockSpec((tm, tn), lambda j, i: (i, j)),
        out_shape=jax.ShapeDtypeStruct((m, n_out), out_dtype),
        scratch_shapes=[pltpu.VMEM((tn, k), BF16)],
        compiler_params=_params(2),
        name=name,
    )(x, wt_all)


def _mla_prep_kernel(x_ref, wt_ref, qg_ref, kvg_ref, wuq_ref, wuk_ref, cos_ref, sa_ref, sb_ref,
                     q_ref, kv_ref, lat_ref, rope_ref, wb_ref, *, n_heads, q_lora, kv_lora, nope, rope_d):
    @pl.when(pl.program_id(0) == 0)
    def _():
        used = wt_ref.shape[1]
        wb_ref[:used] = wt_ref[0].astype(BF16)
        wb_ref[used:] = jnp.zeros((wb_ref.shape[0] - used, wb_ref.shape[1]), BF16)

    zz = _dot_t(x_ref[...], wb_ref[...])
    cos_t, sin_a, sin_b = cos_ref[...], sa_ref[...], sb_ref[...]
    c_q = _rms_norm(zz[:, :q_lora], qg_ref[...])
    q = _dot(c_q.astype(BF16), wuq_ref[...])
    c_kv = _rms_norm(zz[:, q_lora:q_lora + kv_lora], kvg_ref[...])
    lat_ref[...] = c_kv
    kr = _rope_padded(zz[:, q_lora + kv_lora:], cos_t, sin_a, sin_b)
    rope_ref[...] = kr[:, :rope_d]
    kv_ref[:, :kv_lora] = c_kv.astype(BF16)
    kv_ref[:, kv_lora:] = kr.astype(BF16)
    hw = nope + LANES
    for h in range(n_heads):
        q_nope = q[:, h * hw:h * hw + nope]
        q_rot = _rope_padded(q[:, h * hw + nope:(h + 1) * hw], cos_t, sin_a, sin_b)
        q_ref[h, :, :kv_lora] = _dot(q_nope.astype(BF16), wuk_ref[h]).astype(BF16)
        q_ref[h, :, kv_lora:] = q_rot.astype(BF16)


def _mla_prep(xb, wt_all, qg, kvg, wuq, wuk_t, tabs, *, layer, tm, rope_d):
    n, d = xb.shape
    n_heads, nope, kv_lora = wuk_t.shape
    q_lora = wuq.shape[0]
    kw = kv_lora + LANES
    used = q_lora + kv_lora + rope_d
    cos_t, sin_a, sin_b = tabs
    const = lambda shape: pl.BlockSpec(shape, lambda i: (0,) * len(shape))
    row = lambda w: pl.BlockSpec((tm, w), lambda i: (i, 0))
    return pl.pallas_call(
        functools.partial(_mla_prep_kernel, n_heads=n_heads, q_lora=q_lora, kv_lora=kv_lora,
                          nope=nope, rope_d=rope_d),
        grid=(n // tm,),
        in_specs=[row(d), _rows_of(wt_all, layer, 0, used, 1), const(qg.shape), const(kvg.shape), const(wuq.shape),
                  const(wuk_t.shape), row(LANES), row(LANES), row(LANES)],
        out_specs=[pl.BlockSpec((n_heads, tm, kw), lambda i: (0, i, 0)), row(kw), row(kv_lora), row(rope_d)],
        out_shape=[jax.ShapeDtypeStruct((n_heads, n, kw), BF16),
                   jax.ShapeDtypeStruct((n, kw), BF16),
                   jax.ShapeDtypeStruct((n, kv_lora), F32),
                   jax.ShapeDtypeStruct((n, rope_d), F32)],
        scratch_shapes=[pltpu.VMEM((q_lora + kv_lora + LANES, d), BF16)],
        compiler_params=_params(1),
        name="mla_prep",
    )(xb, wt_all, qg, kvg, wuq, wuk_t, cos_t, sin_a, sin_b)


def _mla_prompt_kernel(q_ref, kv_ref, wuv_ref, o_ref, m_ref, l_ref, acc_ref, *, scale, kv_lora):
    n_heads, tq, kw = q_ref.shape
    tk = 2 * tq
    rows = n_heads * tq
    qi = pl.program_id(1)
    q = q_ref[...].reshape(rows, kw)
    m_ref[...] = jnp.full(m_ref.shape, -jnp.inf, F32)
    l_ref[...] = jnp.zeros(l_ref.shape, F32)
    acc_ref[...] = jnp.zeros(acc_ref.shape, F32)

    def block(kb, masked):
        k = kv_ref[pl.ds(pl.multiple_of(kb * tk, tk), tk), :]
        s = _dot_t(q, k) * scale
        if masked:
            q_pos = (lax.broadcasted_iota(jnp.int32, s.shape, 0) & (tq - 1)) + (qi & 1) * tq
            k_pos = lax.broadcasted_iota(jnp.int32, s.shape, 1)
            s = jnp.where(k_pos <= q_pos, s, -jnp.inf)
        m_prev = m_ref[...]
        m_new = jnp.maximum(m_prev, _lane_rep(jnp.max(s, axis=-1, keepdims=True)))
        alpha = jnp.exp(m_prev - m_new)
        p = jnp.exp(s - _lane_tile(m_new, tk))
        l_ref[...] = alpha * l_ref[...] + _lane_rep(jnp.sum(p, axis=-1, keepdims=True))
        acc_ref[...] = _lane_tile(alpha, kv_lora) * acc_ref[...] + _dot(p.astype(BF16), k[:, :kv_lora])
        m_ref[...] = m_new

    def body(kb, carry):
        block(kb, False)
        return carry

    n_full = lax.shift_right_logical(qi, 1)
    lax.fori_loop(0, n_full, body, 0)
    block(n_full, True)
    o = (acc_ref[...] / _lane_tile(l_ref[...], kv_lora)).astype(BF16)
    vh = wuv_ref.shape[2]
    for h in range(n_heads):
        o_ref[:, h * vh:(h + 1) * vh] = _dot(o[h * tq:(h + 1) * tq], wuv_ref[h]).astype(o_ref.dtype)


def _mla_prompt(q_all, kv_all, wuv_t, *, batch, seq, scale):
    n_heads, _, kw = q_all.shape
    _, kv_lora, vh = wuv_t.shape
    tq = MLA_Q_TILE
    nq = seq // tq
    assert tq & (tq - 1) == 0 and seq % (2 * tq) == 0
    return pl.pallas_call(
        functools.partial(_mla_prompt_kernel, scale=scale, kv_lora=kv_lora),
        grid=(batch, nq),
        in_specs=[pl.BlockSpec((n_heads, tq, kw), lambda b, i: (0, b * nq + i, 0)),
                  pl.BlockSpec((seq, kw), lambda b, i: (b, 0)),
                  pl.BlockSpec(wuv_t.shape, lambda b, i: (0, 0, 0))],
        out_specs=pl.BlockSpec((tq, n_heads * vh), lambda b, i: (b * nq + i, 0)),
        out_shape=jax.ShapeDtypeStruct((batch * seq, n_heads * vh), BF16),
        scratch_shapes=[pltpu.VMEM((n_heads * tq, LANES), F32), pltpu.VMEM((n_heads * tq, LANES), F32),
                        pltpu.VMEM((n_heads * tq, kv_lora), F32)],
        compiler_params=_params(2),
        name="mla_prompt",
    )(q_all, kv_all, wuv_t)


def _mla_decode_kernel(pt_ref, q_ref, kvn_ref, lat_hbm, rope_hbm, o_ref,
                       lat_buf, rope_buf, sems, m_ref, l_ref, acc_ref,
                       *, layer, n_chunks, scale, kv_lora, rope_d, dec_seq):
    g = pl.program_id(0)
    n_steps = pl.num_programs(0)
    page = lat_hbm.shape[2]
    bs = n_steps // n_chunks

    def copies(step, slot):
        b = step // n_chunks
        p0 = (step % n_chunks) * DECODE_PAGES
        out = []
        for p in range(DECODE_PAGES):
            idx = pt_ref[(p0 + p) * bs + b]
            keys = pl.ds(p * page, page)
            out.append(pltpu.make_async_copy(lat_hbm.at[layer, idx], lat_buf.at[slot, keys], sems.at[slot, 0]))
            out.append(pltpu.make_async_copy(rope_hbm.at[layer, idx], rope_buf.at[slot, :, keys], sems.at[slot, 1]))
        return out

    @pl.when(g == 0)
    def _():
        for c in copies(0, 0):
            c.start()

    @pl.when(g + 1 < n_steps)
    def _():
        for c in copies(g + 1, (g + 1) % 2):
            c.start()

    slot = g % 2
    for c in copies(g, slot):
        c.wait()

    c_idx = g % n_chunks

    @pl.when(c_idx == 0)
    def _():
        m_ref[...] = jnp.full(m_ref.shape, -jnp.inf, F32)
        l_ref[...] = jnp.zeros(l_ref.shape, F32)
        acc_ref[...] = jnp.zeros(acc_ref.shape, F32)

    q = q_ref[0]

    def accumulate(s, v):
        m_prev = m_ref[...]
        m_new = jnp.maximum(m_prev, _lane_rep(jnp.max(s, axis=-1, keepdims=True)))
        alpha = jnp.exp(m_prev - m_new)
        p = jnp.exp(s - _lane_tile(m_new, s.shape[1]))
        l_ref[...] = alpha * l_ref[...] + _lane_rep(jnp.sum(p, axis=-1, keepdims=True))
        acc_ref[...] = _lane_tile(alpha, kv_lora) * acc_ref[...] + _dot(p.astype(BF16), v)
        m_ref[...] = m_new

    kc = lat_buf[slot].astype(BF16)
    kr_t = rope_buf[slot].astype(BF16)
    s_past = (_dot_t(q[:, :kv_lora], kc) + _dot(q[:, kv_lora:kv_lora + rope_d], kr_t)) * scale
    accumulate(s_past, kc)

    @pl.when(c_idx == n_chunks - 1)
    def _():
        kn = kvn_ref[0]
        s_new = _dot_t(q, kn) * scale
        t_q = lax.broadcasted_iota(jnp.int32, s_new.shape, 0) & (dec_seq - 1)
        t_k = lax.broadcasted_iota(jnp.int32, s_new.shape, 1)
        s_new = jnp.where(t_k <= t_q, s_new, -jnp.inf)
        accumulate(s_new, kn[:, :kv_lora])
        o_ref[0] = (acc_ref[...] / _lane_tile(l_ref[...], kv_lora)).astype(o_ref.dtype)


def _mla_decode(pt_flat, q_s, kvn_s, cache_lat, cache_rope_t, *, layer, scale, dec_seq):
    bs, rows, kw = q_s.shape
    kv_lora = cache_lat.shape[-1]
    rope_d = cache_rope_t.shape[2]
    page = cache_lat.shape[2]
    n_pages = pt_flat.shape[0] // bs
    assert n_pages % DECODE_PAGES == 0
    n_chunks = n_pages // DECODE_PAGES
    grid_spec = pltpu.PrefetchScalarGridSpec(
        num_scalar_prefetch=1,
        grid=(bs * n_chunks,),
        in_specs=[pl.BlockSpec((1, rows, kw), lambda g, pt: (g // n_chunks, 0, 0)),
                  pl.BlockSpec((1,) + kvn_s.shape[1:], lambda g, pt: (g // n_chunks, 0, 0)),
                  pl.BlockSpec(memory_space=pl.ANY),
                  pl.BlockSpec(memory_space=pl.ANY)],
        out_specs=pl.BlockSpec((1, rows, kv_lora), lambda g, pt: (g // n_chunks, 0, 0)),
        scratch_shapes=[pltpu.VMEM((2, DECODE_PAGES * page, kv_lora), F32),
                        pltpu.VMEM((2, rope_d, DECODE_PAGES * page), F32),
                        pltpu.SemaphoreType.DMA((2, 2)),
                        pltpu.VMEM((rows, LANES), F32), pltpu.VMEM((rows, LANES), F32),
                        pltpu.VMEM((rows, kv_lora), F32)],
    )
    return pl.pallas_call(
        functools.partial(_mla_decode_kernel, layer=layer, n_chunks=n_chunks, scale=scale,
                          kv_lora=kv_lora, rope_d=rope_d, dec_seq=dec_seq),
        grid_spec=grid_spec,
        out_shape=jax.ShapeDtypeStruct((bs, rows, kv_lora), BF16),
        compiler_params=_params(1),
        name="mla_decode",
    )(pt_flat, q_s, kvn_s, cache_lat, cache_rope_t)


def _uv_proj_kernel(o_ref, w_ref, y_ref):
    vh = w_ref.shape[2]
    for h in range(o_ref.shape[0]):
        y_ref[:, h * vh:(h + 1) * vh] = _dot(o_ref[h], w_ref[h]).astype(y_ref.dtype)


def _uv_proj(o_heads, wuv_t):
    n_heads, rows, _ = o_heads.shape
    vh = wuv_t.shape[2]
    return pl.pallas_call(
        _uv_proj_kernel,
        out_shape=jax.ShapeDtypeStruct((rows, n_heads * vh), BF16),
        compiler_params=pltpu.CompilerParams(vmem_limit_bytes=VMEM_LIMIT),
        name="uv_proj",
    )(o_heads, wuv_t)


def _ret_tail(o, g_raw, gn_g, gn_b):
    mu = jnp.mean(o, axis=-1, keepdims=True)
    d = o - mu
    var = jnp.mean(d * d, axis=-1, keepdims=True)
    gn = d * lax.rsqrt(var + LN_EPS) * gn_g + gn_b
    return g_raw * _sigmoid(g_raw) * gn


def _ret_prompt_kernel(q_ref, k_ref, v_ref, g_ref, cos_ref, sin_ref, intra_ref, qd_ref, kd_ref, cd_ref,
                       gng_ref, gnb_ref, y_ref, s_ref, *, k_scale):
    c = pl.program_id(1)
    hd = s_ref.shape[-1]

    @pl.when(c == 0)
    def _():
        s_ref[...] = jnp.zeros(s_ref.shape, F32)

    cos, sin = cos_ref[...], sin_ref[...]
    for h in range(N_RET_HEADS):
        cols = slice(h * hd, (h + 1) * hd)
        rq = _rope_halves(q_ref[:, cols], cos, sin)
        rk = _rope_halves(k_ref[:, cols], cos, sin) * k_scale
        qb, kb, vb = rq.astype(BF16), rk.astype(BF16), v_ref[:, cols].astype(BF16)
        state = s_ref[0, h]
        scores = _dot_t(qb, kb) * intra_ref[h]
        o = _dot(scores.astype(BF16), vb) + _dot(qb, state.astype(BF16)) * qd_ref[h]
        kd_t = (rk * kd_ref[h]).T.astype(BF16)
        s_ref[0, h] = state * cd_ref[h] + _dot(kd_t, vb)
        y_ref[:, cols] = _ret_tail(o, g_ref[:, cols], gng_ref[:, cols], gnb_ref[:, cols]).astype(y_ref.dtype)


def _ret_prompt(z, cos_r, sin_r, dec, gn_g, gn_b, *, batch, seq, k_scale):
    width = gn_g.shape[1]
    hd = width // N_RET_HEADS
    nc = seq // RET_CHUNK
    intra, qd, kd, cd = dec
    zcol = lambda j: pl.BlockSpec((RET_CHUNK, width), lambda b, c, j=j: (b * nc + c, j))
    const = lambda a: pl.BlockSpec(a.shape, lambda b, c: (0,) * a.ndim)
    return pl.pallas_call(
        functools.partial(_ret_prompt_kernel, k_scale=k_scale),
        grid=(batch, nc),
        in_specs=[zcol(0), zcol(1), zcol(2), zcol(3),
                  pl.BlockSpec((RET_CHUNK, hd // 2), lambda b, c: (b * nc + c, 0)),
                  pl.BlockSpec((RET_CHUNK, hd // 2), lambda b, c: (b * nc + c, 0)),
                  const(intra), const(qd), const(kd), const(cd), const(gn_g), const(gn_b)],
        out_specs=[pl.BlockSpec((RET_CHUNK, width), lambda b, c: (b * nc + c, 0)),
                   pl.BlockSpec((1, N_RET_HEADS, hd, hd), lambda b, c: (b, 0, 0, 0))],
        out_shape=[jax.ShapeDtypeStruct((batch * seq, width), BF16),
                   jax.ShapeDtypeStruct((batch, N_RET_HEADS, hd, hd), F32)],
        compiler_params=_params(2),
        name="ret_prompt",
    )(z, z, z, z, cos_r, sin_r, intra, qd, kd, cd, gn_g, gn_b)


def _ret_sample_kernel(q_ref, k_ref, v_ref, g_ref, cos_ref, sin_ref, st_ref, intra_ref, qd_ref, kd_ref, cd_ref,
                       gng_ref, gnb_ref, prev_ref, y_ref, so_ref, *, k_scale, dec_seq):
    del prev_ref
    bt = so_ref.shape[1]
    hd = so_ref.shape[-1]
    rows = q_ref.shape[0]
    cos, sin = cos_ref[...], sin_ref[...]
    row_b = _div_pow2(lax.broadcasted_iota(jnp.int32, (rows, hd), 0), dec_seq)
    for h in range(N_RET_HEADS):
        cols = slice(h * hd, (h + 1) * hd)
        rq = _rope_halves(q_ref[:, cols], cos, sin)
        rk = _rope_halves(k_ref[:, cols], cos, sin) * k_scale
        qb, kb, vb = rq.astype(BF16), rk.astype(BF16), v_ref[:, cols].astype(BF16)
        scores = _dot_t(qb, kb) * intra_ref[h]
        o = _dot(scores.astype(BF16), vb)
        kd = rk * kd_ref[h]
        for b in range(bt):
            state = st_ref[0, b, h]
            mine = row_b == b
            o = o + jnp.where(mine, _dot(qb, state.astype(BF16)) * qd_ref[h], 0.0)
            kd_t = jnp.where(mine, kd, 0.0).T.astype(BF16)
            so_ref[0, b, h] = state * cd_ref[h] + _dot(kd_t, vb)
        y_ref[:, cols] = _ret_tail(o, g_ref[:, cols], gng_ref[:, cols], gnb_ref[:, cols]).astype(y_ref.dtype)


def _ret_sample(z, cos_r, sin_r, state_all, new_state_all, dec, gn_g, gn_b, *, layer, row0, bs, dec_seq, bt, k_scale):
    width = gn_g.shape[1]
    hd = width // N_RET_HEADS
    rows = bt * dec_seq
    r0 = row0 // rows
    intra, qd, kd, cd = dec
    zcol = lambda j: pl.BlockSpec((rows, width), lambda i, j=j: (r0 + i, j))
    const = lambda a: pl.BlockSpec(a.shape, lambda i: (0,) * a.ndim)
    state_blk = pl.BlockSpec((1, bt, N_RET_HEADS, hd, hd), lambda i: (layer, i, 0, 0, 0))
    in_specs = [zcol(0), zcol(1), zcol(2), zcol(3),
                pl.BlockSpec((rows, hd // 2), lambda i: (r0 + i, 0)),
                pl.BlockSpec((rows, hd // 2), lambda i: (r0 + i, 0)),
                state_blk, const(intra), const(qd), const(kd), const(cd), const(gn_g), const(gn_b),
                pl.BlockSpec(memory_space=pl.ANY)]
    args = [z, z, z, z, cos_r, sin_r, state_all, intra, qd, kd, cd, gn_g, gn_b, new_state_all]
    aliases = {len(args) - 1: 1}
    return pl.pallas_call(
        functools.partial(_ret_sample_kernel, k_scale=k_scale, dec_seq=dec_seq),
        grid=(bs // bt,),
        in_specs=in_specs,
        out_specs=[pl.BlockSpec((rows, width), lambda i: (i, 0)), state_blk],
        out_shape=[jax.ShapeDtypeStruct((bs * dec_seq, width), BF16),
                   jax.ShapeDtypeStruct(state_all.shape, F32)],
        input_output_aliases=aliases,
        compiler_params=_params(1),
        name="ret_sample",
    )(*args)


def _softmax_rows(s):
    e = jnp.exp(s - jnp.max(s, axis=-1, keepdims=True))
    return e / jnp.sum(e, axis=-1, keepdims=True)


def _mem_prompt_kernel(q_ref, kv_ref, o_ref, *, scale):
    width = o_ref.shape[1]
    hd = width // N_MEM_HEADS
    for h in range(N_MEM_HEADS):
        cols = slice(h * hd, (h + 1) * hd)
        k = kv_ref[:, cols].astype(BF16)
        v = kv_ref[:, width + h * hd:width + (h + 1) * hd].astype(BF16)
        p = _softmax_rows(_dot_t(q_ref[:, cols].astype(BF16), k) * scale)
        o_ref[:, cols] = _dot(p.astype(BF16), v).astype(o_ref.dtype)


def _mem_prompt(z, mkv, *, batch, seq, n_mem, width, qcol, tm, scale):
    nt = seq // tm
    return pl.pallas_call(
        functools.partial(_mem_prompt_kernel, scale=scale),
        grid=(batch, nt),
        in_specs=[pl.BlockSpec((tm, width), lambda b, i: (b * nt + i, qcol)),
                  pl.BlockSpec((n_mem, 2 * width), lambda b, i: (b, 0))],
        out_specs=pl.BlockSpec((tm, width), lambda b, i: (b * nt + i, 0)),
        out_shape=jax.ShapeDtypeStruct((batch * seq, width), BF16),
        compiler_params=_params(2),
        name="mem_prompt",
    )(z, mkv)


def _mem_sample_kernel(q_ref, k_ref, v_ref, o_ref, *, scale, dec_seq):
    bt = k_ref.shape[1]
    width = o_ref.shape[1]
    hd = width // N_MEM_HEADS
    rows = q_ref.shape[0]
    row_b = _div_pow2(lax.broadcasted_iota(jnp.int32, (rows, hd), 0), dec_seq)
    for h in range(N_MEM_HEADS):
        cols = slice(h * hd, (h + 1) * hd)
        qb = q_ref[:, cols].astype(BF16)
        o = jnp.zeros((rows, hd), F32)
        for b in range(bt):
            p = _softmax_rows(_dot_t(qb, k_ref[0, b, :, cols].astype(BF16)) * scale)
            o = jnp.where(row_b == b, _dot(p.astype(BF16), v_ref[0, b, :, cols].astype(BF16)), o)
        o_ref[:, cols] = o.astype(o_ref.dtype)


def _mem_sample(z, ck, cv, *, layer, row0, bs, dec_seq, bt, width, qcol, scale):
    rows = bt * dec_seq
    r0 = row0 // rows
    n_mem = ck.shape[2]
    cache = pl.BlockSpec((1, bt, n_mem, width), lambda i: (layer, i, 0, 0))
    return pl.pallas_call(
        functools.partial(_mem_sample_kernel, scale=scale, dec_seq=dec_seq),
        grid=(bs // bt,),
        in_specs=[pl.BlockSpec((rows, width), lambda i: (r0 + i, qcol)), cache, cache],
        out_specs=pl.BlockSpec((rows, width), lambda i: (i, 0)),
        out_shape=jax.ShapeDtypeStruct((bs * dec_seq, width), BF16),
        compiler_params=_params(1),
        name="mem_sample",
    )(z, ck, cv)


def _merge_kernel(ya_ref, yb_ref, yc_ref, ga_ref, gb_ref, gc_ref, wa_ref, wb_ref, wc_ref, o_ref):
    acc = _sigmoid(ga_ref[...]) * _dot(ya_ref[...], wa_ref[...])
    acc = acc + _sigmoid(gb_ref[...]) * _dot(yb_ref[...], wb_ref[...])
    acc = acc + _sigmoid(gc_ref[...]) * _dot(yc_ref[...], wc_ref[...])
    o_ref[...] = acc.astype(o_ref.dtype)


def _merge(ya, yb, yc, z, wa, wb, wc, *, layer, gate_col0, tm, tn):
    n, width = ya.shape
    d = wa.shape[2]
    nj = d // tn
    g0 = gate_col0 // tn
    y = pl.BlockSpec((tm, width), lambda j, i: (i, 0))
    w = _layer_block(wa, layer, (width, tn), lambda j, i: (0, j))
    gate = lambda k: pl.BlockSpec((tm, tn), lambda j, i, k=k: (i, g0 + k * nj + j))
    return pl.pallas_call(
        _merge_kernel,
        grid=(nj, n // tm),
        in_specs=[y, y, y, gate(0), gate(1), gate(2), w, w, w],
        out_specs=pl.BlockSpec((tm, tn), lambda j, i: (i, j)),
        out_shape=jax.ShapeDtypeStruct((n, d), BF16),
        compiler_params=_params(2),
        name="merge",
    )(ya, yb, yc, z, z, z, wa, wb, wc)


def _oproj_ln_kernel(m_ref, w_ref, x_ref, g_ref, b_ref, y_ref, yb_ref, *, alpha):
    y = _layer_norm(alpha * x_ref[...] + _dot(m_ref[...], w_ref[...]), g_ref[...], b_ref[...])
    y_ref[...] = y
    yb_ref[...] = y.astype(BF16)


def _oproj_ln(merged, w_o, x, g, b, *, layer, alpha, tm):
    n, d = x.shape
    row = pl.BlockSpec((tm, d), lambda i: (i, 0))
    const = lambda a: pl.BlockSpec(a.shape, lambda i: (0,) * a.ndim)
    return pl.pallas_call(
        functools.partial(_oproj_ln_kernel, alpha=alpha),
        grid=(n // tm,),
        in_specs=[row, _layer_block(w_o, layer, w_o.shape[1:], lambda i: (0, 0)), row, const(g), const(b)],
        out_specs=[row, row],
        out_shape=[jax.ShapeDtypeStruct((n, d), F32), jax.ShapeDtypeStruct((n, d), BF16)],
        compiler_params=_params(1),
        name="oproj_ln",
    )(merged, w_o, x, g, b)


def _ffn_down_ln_kernel(h_ref, w_ref, x_ref, g_ref, b_ref, y_ref, yb_ref, acc_ref, *, alpha):
    k = pl.program_id(1)
    part = _dot(h_ref[...], w_ref[...])

    @pl.when(k == 0)
    def _():
        acc_ref[...] = part

    @pl.when(k > 0)
    def _():
        acc_ref[...] += part

    @pl.when(k == pl.num_programs(1) - 1)
    def _():
        y = _layer_norm(alpha * x_ref[...] + acc_ref[...], g_ref[...], b_ref[...])
        y_ref[...] = y
        yb_ref[...] = y.astype(BF16)


def _ffn_down_ln(h, w_down, x, g, b, *, layer, alpha, tm, tk):
    n, d = x.shape
    ff = h.shape[1]
    row = pl.BlockSpec((tm, d), lambda i, k: (i, 0))
    const = lambda a: pl.BlockSpec(a.shape, lambda i, k: (0,) * a.ndim)
    return pl.pallas_call(
        functools.partial(_ffn_down_ln_kernel, alpha=alpha),
        grid=(n // tm, ff // tk),
        in_specs=[pl.BlockSpec((tm, tk), lambda i, k: (i, k)),
                  _layer_block(w_down, layer, (tk, d), lambda i, k: (k, 0)),
                  row, const(g), const(b)],
        out_specs=[row, row],
        out_shape=[jax.ShapeDtypeStruct((n, d), F32), jax.ShapeDtypeStruct((n, d), BF16)],
        scratch_shapes=[pltpu.VMEM((tm, d), F32)],
        compiler_params=_params(2),
        name="ffn_down_ln",
    )(h, w_down, x, g, b)


def _rope_tables(pos, rope_d, ret_hd):
    posf = pos.astype(F32)[:, None]

    def cos_sin(d):
        half = d // 2
        inv = jnp.power(ROPE_BASE, -jnp.arange(half, dtype=F32) * (2.0 / d))
        ang = posf * inv[None, :]
        return jnp.cos(ang), jnp.sin(ang)

    cos, sin = cos_sin(rope_d)
    zero = jnp.zeros_like(cos)
    pad = jnp.zeros((cos.shape[0], LANES - 2 * cos.shape[1]), F32)
    mla = (jnp.concatenate([cos, cos, pad], 1),
           jnp.concatenate([-sin, zero, pad], 1),
           jnp.concatenate([zero, sin, pad], 1))
    return mla, cos_sin(ret_hd)


def _decay_tables(log_gamma, chunk, groups, hd):
    idx = jnp.arange(chunk, dtype=F32)
    lg = log_gamma[:, None]
    diff = idx[:, None] - idx[None, :]
    intra = jnp.where(diff >= 0, jnp.exp(lg[:, :, None] * jnp.maximum(diff, 0.0)), 0.0)
    q_decay = jnp.exp(lg * (idx + 1.0))
    k_decay = jnp.exp(lg * (chunk - 1.0 - idx))
    chunk_decay = jnp.exp(log_gamma * chunk)
    nh = log_gamma.shape[0]
    eye = jnp.eye(groups, dtype=F32)
    intra = (eye[None, :, None, :, None] * intra[:, None, :, None, :]).reshape(nh, groups * chunk, groups * chunk)
    q_decay = jnp.broadcast_to(jnp.tile(q_decay, (1, groups))[:, :, None], (nh, groups * chunk, hd))
    k_decay = jnp.broadcast_to(jnp.tile(k_decay, (1, groups))[:, :, None], (nh, groups * chunk, hd))
    chunk_decay = jnp.broadcast_to(chunk_decay[:, None, None], (nh, 1, hd))
    return intra, q_decay, k_decay, chunk_decay


def kernel(x_prompt, x_sample, mem_prompt, cache_mla_latent, cache_mla_rope, page_table, state_ret, cache_mem_k, cache_mem_v, w_in, q_norm_g, kv_norm_g, w_uq, w_uk, w_uv, ret_gn_g, ret_gn_b, w_mem_k, w_mem_v, w_br_mla, w_br_ret, w_br_mem, w_o, ln1_g, ln1_b, w_up, w_down, ln2_g, ln2_b):
    depth = w_in.shape[0]
    batch, seq, d_model = x_prompt.shape
    bs, dec_seq, _ = x_sample.shape
    n_mem = mem_prompt.shape[1]
    kv_lora, n_heads, nope = w_uk.shape[1:]
    q_lora = w_uq.shape[1]
    rope_d = cache_mla_rope.shape[-1]
    ret_width = ret_gn_g.shape[1]
    ret_hd = ret_width // N_RET_HEADS
    mem_width = w_mem_k.shape[2]
    d_ff = w_up.shape[2]
    page = cache_mla_latent.shape[2]
    past = page_table.shape[1] * page
    n_p = batch * seq
    n_s = bs * dec_seq
    n = n_p + n_s
    alpha = (2 * depth) ** 0.25
    mla_scale = (nope + rope_d) ** -0.5
    assert 2 * rope_d <= LANES and (rope_d // 2) * 4 == LANES and ret_hd // 2 == LANES
    off_kv_end = q_lora + kv_lora + rope_d

    memq_col = 4 * ret_width
    gate_col = memq_col + mem_width

    pos = jnp.concatenate([jnp.arange(seq, dtype=jnp.int32)] * batch
                          + [past + jnp.arange(dec_seq, dtype=jnp.int32)] * bs)
    mla_tabs, (cos_r, sin_r) = _rope_tables(pos, rope_d, ret_hd)
    log_gamma = jnp.log(1.0 - jnp.power(2.0, -5.0 - jnp.arange(N_RET_HEADS, dtype=F32)))
    bt = 16 // dec_seq
    dec_p = _decay_tables(log_gamma, RET_CHUNK, 1, ret_hd)
    dec_s = _decay_tables(log_gamma, dec_seq, bt, ret_hd)

    tm_big = _tile(n, 1088)
    tm_mid = _tile(n, 544)
    tm_small = _tile(n, 256)
    tm_prep = _tile(n, 272)
    tn = 512 if d_model % 512 == 0 else _tile(d_model, 512, LANES)

    x = jnp.concatenate([x_prompt.reshape(n_p, d_model), x_sample.reshape(n_s, d_model)], axis=0)
    xb = x.astype(BF16)
    mem_b = mem_prompt.reshape(batch * n_mem, d_model).astype(BF16)
    wt_in = jnp.swapaxes(w_in, 1, 2)
    cache_rope_t = jnp.swapaxes(cache_mla_rope, 2, 3)
    pt_flat = page_table.T.reshape(-1)
    z_width = w_in.shape[2] - off_kv_end
    wb_memkv = jnp.concatenate([w_mem_k, w_mem_v], axis=2).astype(BF16)
    wb_mla, wb_ret, wb_mem = w_br_mla.astype(BF16), w_br_ret.astype(BF16), w_br_mem.astype(BF16)
    wb_o, wb_up, wb_down = w_o.astype(BF16), w_up.astype(BF16), w_down.astype(BF16)

    outs = [[] for _ in range(7)]
    new_state_s = jnp.zeros(state_ret.shape, F32)
    for l in range(depth):
        wq = w_uq[l].reshape(q_lora, n_heads, nope + rope_d)
        wq = jnp.pad(wq, ((0, 0), (0, 0), (0, LANES - rope_d))).reshape(q_lora, n_heads * (nope + LANES)).astype(BF16)
        wuk_t = jnp.transpose(w_uk[l], (1, 2, 0)).astype(BF16)
        wuv_t = jnp.transpose(w_uv[l], (1, 0, 2)).astype(BF16)

        z = _matmul_wt(xb, wt_in, F32, layer=l, row0=off_kv_end, n_out=z_width, tm=tm_big,
                       tn=_tile(z_width, 1024, LANES), name="in_proj")
        q_all, kv_all, lat, kro = _mla_prep(xb, wt_in, q_norm_g[l][None], kv_norm_g[l][None], wq, wuk_t, mla_tabs,
                                            layer=l, tm=tm_prep, rope_d=rope_d)

        y_mla_p = _mla_prompt(q_all, kv_all, wuv_t, batch=batch, seq=seq, scale=mla_scale)
        q_s = q_all[:, n_p:].reshape(n_heads, bs, dec_seq, -1).transpose(1, 0, 2, 3).reshape(bs, n_heads * dec_seq, -1)
        kvn_s = jnp.pad(kv_all[n_p:].reshape(bs, dec_seq, -1), ((0, 0), (0, LANES - dec_seq), (0, 0)))
        o_lat_s = _mla_decode(pt_flat, q_s, kvn_s, cache_mla_latent, cache_rope_t, layer=l, scale=mla_scale,
                              dec_seq=dec_seq)
        o_heads = o_lat_s.reshape(bs, n_heads, dec_seq, kv_lora).transpose(1, 0, 2, 3).reshape(n_heads, n_s, kv_lora)
        y_mla = jnp.concatenate([y_mla_p, _uv_proj(o_heads, wuv_t)], axis=0)

        gn_g, gn_b = ret_gn_g[l][None], ret_gn_b[l][None]
        y_ret_p, s_p = _ret_prompt(z, cos_r, sin_r, dec_p, gn_g, gn_b, batch=batch, seq=seq, k_scale=ret_hd ** -0.5)
        y_ret_s, new_state_s = _ret_sample(z, cos_r, sin_r, state_ret, new_state_s, dec_s, gn_g, gn_b, layer=l,
                                           row0=n_p, bs=bs, dec_seq=dec_seq, bt=bt, k_scale=ret_hd ** -0.5)
        y_ret = jnp.concatenate([y_ret_p, y_ret_s], axis=0)

        mkv = _matmul(mem_b, wb_memkv, F32, layer=l, tm=_tile(batch * n_mem, 512), tn=tn, name="mem_kv")
        mem_scale = (mem_width // N_MEM_HEADS) ** -0.5
        o_mem_p = _mem_prompt(z, mkv, batch=batch, seq=seq, n_mem=n_mem, width=mem_width,
                              qcol=memq_col // mem_width, tm=_tile(seq, 512), scale=mem_scale)
        o_mem_s = _mem_sample(z, cache_mem_k, cache_mem_v, layer=l, row0=n_p, bs=bs, dec_seq=dec_seq, bt=bt,
                              width=mem_width, qcol=memq_col // mem_width, scale=mem_scale)
        o_mem = jnp.concatenate([o_mem_p, o_mem_s], axis=0)

        merged = _merge(y_mla, y_ret, o_mem, z, wb_mla, wb_ret, wb_mem, layer=l, gate_col0=gate_col, tm=tm_mid,
                        tn=_tile(d_model, 1024, LANES))
        x1, x1b = _oproj_ln(merged, wb_o, x, ln1_g[l][None], ln1_b[l][None], layer=l, alpha=alpha, tm=tm_small)
        hmid = _matmul(x1b, wb_up, BF16, layer=l, tm=tm_big, tn=tn, epilogue="relu_sq", name="ffn_up")
        x, xb = _ffn_down_ln(hmid, wb_down, x1, ln2_g[l][None], ln2_b[l][None], layer=l, alpha=alpha,
                             tm=tm_mid, tk=_tile(d_ff, 1024, LANES))

        outs[0].append(lat[:n_p].reshape(batch, seq, kv_lora))
        outs[1].append(kro[:n_p].reshape(batch, seq, rope_d))
        outs[2].append(s_p)
        outs[3].append(mkv[:, :mem_width].reshape(batch, n_mem, mem_width))
        outs[4].append(mkv[:, mem_width:].reshape(batch, n_mem, mem_width))
        outs[5].append(lat[n_p:].reshape(bs, dec_seq, kv_lora))
        outs[6].append(kro[n_p:].reshape(bs, dec_seq, rope_d))

    return (x[:n_p].reshape(batch, seq, d_model), x[n_p:].reshape(bs, dec_seq, d_model),
            *[jnp.stack(o) for o in outs], new_state_s)
```

```python
import functools
import math

import jax
import jax.numpy as jnp
from jax import lax
from jax.experimental import pallas as pl
from jax.experimental.pallas import tpu as pltpu

F32 = jnp.float32
BF16 = jnp.bfloat16

N_RET_HEADS = 4
N_MEM_HEADS = 4
RET_CHUNK = 128
LN_EPS = 1e-5
RMS_EPS = 1e-6
ROPE_BASE = 10000.0
LANES = 128
VMEM_LIMIT = 52 * 1024 * 1024
MLA_Q_TILE = 256
DECODE_PAGES = 16


def _params(n_axes, vmem=VMEM_LIMIT):
    return pltpu.CompilerParams(dimension_semantics=("arbitrary",) * n_axes, vmem_limit_bytes=vmem)


def _tile(n, target, mult=16):
    best = None
    for t in range(mult, min(n, target) + 1, mult):
        if n % t == 0:
            best = t
    assert best is not None, (n, target, mult)
    return best


def _dot(a, b):
    return jnp.dot(a, b, preferred_element_type=F32)


def _dot_t(a, b):
    return lax.dot_general(a, b, (((1,), (1,)), ((), ())), preferred_element_type=F32)


def _div_pow2(x, d):
    assert d & (d - 1) == 0
    return lax.shift_right_logical(x, jnp.int32(d.bit_length() - 1))


def _lane_rep(col):
    return jnp.broadcast_to(col, (col.shape[0], LANES))


def _lane_tile(a, width):
    return jnp.concatenate([a] * (width // LANES), axis=1)


def _sigmoid(x):
    return 1.0 / (1.0 + jnp.exp(-x))


def _rms_norm(x, g):
    return x * lax.rsqrt(jnp.mean(x * x, axis=-1, keepdims=True) + RMS_EPS) * g


def _layer_norm(u, g, b):
    mu = jnp.mean(u, axis=-1, keepdims=True)
    d = u - mu
    var = jnp.mean(d * d, axis=-1, keepdims=True)
    return d * lax.rsqrt(var + LN_EPS) * g + b


def _rope_padded(v, cos_t, sin_a, sin_b):
    q = LANES // 4
    return v * cos_t + pltpu.roll(v, LANES - q, 1) * sin_a + pltpu.roll(v, q, 1) * sin_b


def _rope_halves(x, cos, sin):
    h = x.shape[-1] // 2
    x1, x2 = x[:, :h], x[:, h:]
    return jnp.concatenate([x1 * cos - x2 * sin, x2 * cos + x1 * sin], axis=-1)


def _mm_kernel(x_ref, w_ref, o_ref, *, epilogue):
    acc = _dot(x_ref[...], w_ref[...])
    if epilogue == "relu_sq":
        acc = jnp.square(jnp.maximum(acc, 0.0))
    o_ref[...] = acc.astype(o_ref.dtype)


def _layer_block(w_all, layer, block, index):
    return pl.BlockSpec((None,) + block, lambda *g: (layer,) + index(*g))


def _matmul(x, w_all, out_dtype, *, layer, tm, tn, epilogue=None, name):
    m, k = x.shape
    n = w_all.shape[2]
    return pl.pallas_call(
        functools.partial(_mm_kernel, epilogue=epilogue),
        grid=(n // tn, m // tm),
        in_specs=[pl.BlockSpec((tm, k), lambda j, i: (i, 0)),
                  _layer_block(w_all, layer, (k, tn), lambda j, i: (0, j))],
        out_specs=pl.BlockSpec((tm, tn), lambda j, i: (i, j)),
        out_shape=jax.ShapeDtypeStruct((m, n), out_dtype),
        compiler_params=_params(2),
        name=name,
    )(x, w_all)


def _rows_of(wt_all, layer, row0, rows, index_args):
    k = wt_all.shape[2]
    if index_args == 2:
        imap = lambda j, i: (layer, pl.multiple_of(row0 + j * rows, 8), 0)
    else:
        imap = lambda i: (layer, row0, 0)
    return pl.BlockSpec((pl.Element(1), pl.Element(rows), pl.Element(k)), imap)


def _mm_wt_kernel(x_ref, wt_ref, o_ref, wb_ref):
    @pl.when(pl.program_id(1) == 0)
    def _():
        wb_ref[...] = wt_ref[0].astype(BF16)

    o_ref[...] = _dot_t(x_ref[...], wb_ref[...]).astype(o_ref.dtype)


def _matmul_wt(x, wt_all, out_dtype, *, layer, row0, n_out, tm, tn, name):
    m, k = x.shape
    assert row0 % 8 == 0 and n_out % tn == 0
    return pl.pallas_call(
        _mm_wt_kernel,
        grid=(n_out // tn, m // tm),
        in_specs=[pl.BlockSpec((tm, k), lambda j, i: (i, 0)), _rows_of(wt_all, layer, row0, tn, 2)],
        out_specs=pl.BlockSpec((tm, tn), lambda j, i: (i, j)),
        out_shape=jax.ShapeDtypeStruct((m, n_out), out_dtype),
        scratch_shapes=[pltpu.VMEM((tn, k), BF16)],
        compiler_params=_params(2),
        name=name,
    )(x, wt_all)


def _mla_prep_kernel(x_ref, wt_ref, qg_ref, kvg_ref, wuq_ref, wuk_ref, cos_ref, sa_ref, sb_ref,
                     q_ref, kv_ref, lat_ref, rope_ref, wb_ref, *, n_heads, q_lora, kv_lora, nope, rope_d):
    @pl.when(pl.program_id(0) == 0)
    def _():
        used = wt_ref.shape[1]
        wb_ref[:used] = wt_ref[0].astype(BF16)
        wb_ref[used:] = jnp.zeros((wb_ref.shape[0] - used, wb_ref.shape[1]), BF16)

    zz = _dot_t(x_ref[...], wb_ref[...])
    cos_t, sin_a, sin_b = cos_ref[...], sa_ref[...], sb_ref[...]
    c_q = _rms_norm(zz[:, :q_lora], qg_ref[...])
    q = _dot(c_q.astype(BF16), wuq_ref[...])
    c_kv = _rms_norm(zz[:, q_lora:q_lora + kv_lora], kvg_ref[...])
    lat_ref[...] = c_kv
    kr = _rope_padded(zz[:, q_lora + kv_lora:], cos_t, sin_a, sin_b)
    rope_ref[...] = kr[:, :rope_d]
    kv_ref[:, :kv_lora] = c_kv.astype(BF16)
    kv_ref[:, kv_lora:] = kr.astype(BF16)
    hw = nope + LANES
    for h in range(n_heads):
        q_nope = q[:, h * hw:h * hw + nope]
        q_rot = _rope_padded(q[:, h * hw + nope:(h + 1) * hw], cos_t, sin_a, sin_b)
        q_ref[h, :, :kv_lora] = _dot(q_nope.astype(BF16), wuk_ref[h]).astype(BF16)
        q_ref[h, :, kv_lora:] = q_rot.astype(BF16)


def _mla_prep(xb, wt_all, qg, kvg, wuq, wuk_t, tabs, *, layer, tm, rope_d):
    n, d = xb.shape
    n_heads, nope, kv_lora = wuk_t.shape
    q_lora = wuq.shape[0]
    kw = kv_lora + LANES
    used = q_lora + kv_lora + rope_d
    cos_t, sin_a, sin_b = tabs
    const = lambda shape: pl.BlockSpec(shape, lambda i: (0,) * len(shape))
    row = lambda w: pl.BlockSpec((tm, w), lambda i: (i, 0))
    return pl.pallas_call(
        functools.partial(_mla_prep_kernel, n_heads=n_heads, q_lora=q_lora, kv_lora=kv_lora,
                          nope=nope, rope_d=rope_d),
        grid=(n // tm,),
        in_specs=[row(d), _rows_of(wt_all, layer, 0, used, 1), const(qg.shape), const(kvg.shape), const(wuq.shape),
                  const(wuk_t.shape), row(LANES), row(LANES), row(LANES)],
        out_specs=[pl.BlockSpec((n_heads, tm, kw), lambda i: (0, i, 0)), row(kw), row(kv_lora), row(rope_d)],
        out_shape=[jax.ShapeDtypeStruct((n_heads, n, kw), BF16),
                   jax.ShapeDtypeStruct((n, kw), BF16),
                   jax.ShapeDtypeStruct((n, kv_lora), F32),
                   jax.ShapeDtypeStruct((n, rope_d), F32)],
        scratch_shapes=[pltpu.VMEM((q_lora + kv_lora + LANES, d), BF16)],
        compiler_params=_params(1),
        name="mla_prep",
    )(xb, wt_all, qg, kvg, wuq, wuk_t, cos_t, sin_a, sin_b)


def _mla_prompt_kernel(q_ref, kv_ref, wuv_ref, o_ref, m_ref, l_ref, acc_ref, *, scale, kv_lora):
    n_heads, tq, kw = q_ref.shape
    tk = 2 * tq
    rows = n_heads * tq
    qi = pl.program_id(1)
    q = q_ref[...].reshape(rows, kw)
    m_ref[...] = jnp.full(m_ref.shape, -jnp.inf, F32)
    l_ref[...] = jnp.zeros(l_ref.shape, F32)
    acc_ref[...] = jnp.zeros(acc_ref.shape, F32)

    def block(kb, masked):
        k = kv_ref[pl.ds(pl.multiple_of(kb * tk, tk), tk), :]
        s = _dot_t(q, k) * scale
        if masked:
            q_pos = (lax.broadcasted_iota(jnp.int32, s.shape, 0) & (tq - 1)) + (qi & 1) * tq
            k_pos = lax.broadcasted_iota(jnp.int32, s.shape, 1)
            s = jnp.where(k_pos <= q_pos, s, -jnp.inf)
        m_prev = m_ref[...]
        m_new = jnp.maximum(m_prev, _lane_rep(jnp.max(s, axis=-1, keepdims=True)))
        alpha = jnp.exp(m_prev - m_new)
        p = jnp.exp(s - _lane_tile(m_new, tk))
        l_ref[...] = alpha * l_ref[...] + _lane_rep(jnp.sum(p, axis=-1, keepdims=True))
        acc_ref[...] = _lane_tile(alpha, kv_lora) * acc_ref[...] + _dot(p.astype(BF16), k[:, :kv_lora])
        m_ref[...] = m_new

    def body(kb, carry):
        block(kb, False)
        return carry

    n_full = lax.shift_right_logical(qi, 1)
    lax.fori_loop(0, n_full, body, 0)
    block(n_full, True)
    o = (acc_ref[...] / _lane_tile(l_ref[...], kv_lora)).astype(BF16)
    vh = wuv_ref.shape[2]
    for h in range(n_heads):
        o_ref[:, h * vh:(h + 1) * vh] = _dot(o[h * tq:(h + 1) * tq], wuv_ref[h]).astype(o_ref.dtype)


def _mla_prompt(q_all, kv_all, wuv_t, *, batch, seq, scale):
    n_heads, _, kw = q_all.shape
    _, kv_lora, vh = wuv_t.shape
    tq = MLA_Q_TILE
    nq = seq // tq
    assert tq & (tq - 1) == 0 and seq % (2 * tq) == 0
    return pl.pallas_call(
        functools.partial(_mla_prompt_kernel, scale=scale, kv_lora=kv_lora),
        grid=(batch, nq),
        in_specs=[pl.BlockSpec((n_heads, tq, kw), lambda b, i: (0, b * nq + i, 0)),
                  pl.BlockSpec((seq, kw), lambda b, i: (b, 0)),
                  pl.BlockSpec(wuv_t.shape, lambda b, i: (0, 0, 0))],
        out_specs=pl.BlockSpec((tq, n_heads * vh), lambda b, i: (b * nq + i, 0)),
        out_shape=jax.ShapeDtypeStruct((batch * seq, n_heads * vh), BF16),
        scratch_shapes=[pltpu.VMEM((n_heads * tq, LANES), F32), pltpu.VMEM((n_heads * tq, LANES), F32),
                        pltpu.VMEM((n_heads * tq, kv_lora), F32)],
        compiler_params=_params(2),
        name="mla_prompt",
    )(q_all, kv_all, wuv_t)


def _mla_decode_kernel(pt_ref, q_ref, kvn_ref, lat_hbm, rope_hbm, o_ref,
                       lat_buf, rope_buf, sems, m_ref, l_ref, acc_ref,
                       *, layer, n_chunks, scale, kv_lora, rope_d, dec_seq):
    g = pl.program_id(0)
    n_steps = pl.num_programs(0)
    page = lat_hbm.shape[2]
    bs = n_steps // n_chunks

    def copies(step, slot):
        b = step // n_chunks
        p0 = (step % n_chunks) * DECODE_PAGES
        out = []
        for p in range(DECODE_PAGES):
            idx = pt_ref[(p0 + p) * bs + b]
            keys = pl.ds(p * page, page)
            out.append(pltpu.make_async_copy(lat_hbm.at[layer, idx], lat_buf.at[slot, keys], sems.at[slot, 0]))
            out.append(pltpu.make_async_copy(rope_hbm.at[layer, idx], rope_buf.at[slot, :, keys], sems.at[slot, 1]))
        return out

    def start(step, slot):
        for n_c, c in enumerate(copies(step, slot)):
            c.start(priority=(n_c // 2) % 2)

    @pl.when(g == 0)
    def _():
        start(0, 0)

    @pl.when(g + 1 < n_steps)
    def _():
        start(g + 1, (g + 1) % 2)

    slot = g % 2
    for c in copies(g, slot):
        c.wait()

    c_idx = g % n_chunks

    @pl.when(c_idx == 0)
    def _():
        m_ref[...] = jnp.full(m_ref.shape, -jnp.inf, F32)
        l_ref[...] = jnp.zeros(l_ref.shape, F32)
        acc_ref[...] = jnp.zeros(acc_ref.shape, F32)

    q = q_ref[0]

    def accumulate(s, v):
        m_prev = m_ref[...]
        m_new = jnp.maximum(m_prev, _lane_rep(jnp.max(s, axis=-1, keepdims=True)))
        alpha = jnp.exp(m_prev - m_new)
        p = jnp.exp(s - _lane_tile(m_new, s.shape[1]))
        l_ref[...] = alpha * l_ref[...] + _lane_rep(jnp.sum(p, axis=-1, keepdims=True))
        acc_ref[...] = _lane_tile(alpha, kv_lora) * acc_ref[...] + _dot(p.astype(BF16), v)
        m_ref[...] = m_new

    kc = lat_buf[slot].astype(BF16)
    kr_t = rope_buf[slot].astype(BF16)
    s_past = (_dot_t(q[:, :kv_lora], kc) + _dot(q[:, kv_lora:kv_lora + rope_d], kr_t)) * scale
    accumulate(s_past, kc)

    @pl.when(c_idx == n_chunks - 1)
    def _():
        kn = kvn_ref[0]
        s_new = _dot_t(q, kn) * scale
        t_q = lax.broadcasted_iota(jnp.int32, s_new.shape, 0) & (dec_seq - 1)
        t_k = lax.broadcasted_iota(jnp.int32, s_new.shape, 1)
        s_new = jnp.where(t_k <= t_q, s_new, -jnp.inf)
        accumulate(s_new, kn[:, :kv_lora])
        o_ref[0] = (acc_ref[...] / _lane_tile(l_ref[...], kv_lora)).astype(o_ref.dtype)


def _mla_decode(pt_flat, q_s, kvn_s, cache_lat, cache_rope_t, *, layer, scale, dec_seq):
    bs, rows, kw = q_s.shape
    kv_lora = cache_lat.shape[-1]
    rope_d = cache_rope_t.shape[2]
    page = cache_lat.shape[2]
    n_pages = pt_flat.shape[0] // bs
    assert n_pages % DECODE_PAGES == 0
    n_chunks = n_pages // DECODE_PAGES
    grid_spec = pltpu.PrefetchScalarGridSpec(
        num_scalar_prefetch=1,
        grid=(bs * n_chunks,),
        in_specs=[pl.BlockSpec((1, rows, kw), lambda g, pt: (g // n_chunks, 0, 0)),
                  pl.BlockSpec((1,) + kvn_s.shape[1:], lambda g, pt: (g // n_chunks, 0, 0)),
                  pl.BlockSpec(memory_space=pl.ANY),
                  pl.BlockSpec(memory_space=pl.ANY)],
        out_specs=pl.BlockSpec((1, rows, kv_lora), lambda g, pt: (g // n_chunks, 0, 0)),
        scratch_shapes=[pltpu.VMEM((2, DECODE_PAGES * page, kv_lora), F32),
                        pltpu.VMEM((2, rope_d, DECODE_PAGES * page), F32),
                        pltpu.SemaphoreType.DMA((2, 2)),
                        pltpu.VMEM((rows, LANES), F32), pltpu.VMEM((rows, LANES), F32),
                        pltpu.VMEM((rows, kv_lora), F32)],
    )
    return pl.pallas_call(
        functools.partial(_mla_decode_kernel, layer=layer, n_chunks=n_chunks, scale=scale,
                          kv_lora=kv_lora, rope_d=rope_d, dec_seq=dec_seq),
        grid_spec=grid_spec,
        out_shape=jax.ShapeDtypeStruct((bs, rows, kv_lora), BF16),
        compiler_params=_params(1),
        name="mla_decode",
    )(pt_flat, q_s, kvn_s, cache_lat, cache_rope_t)


def _uv_proj_kernel(o_ref, w_ref, y_ref):
    vh = w_ref.shape[2]
    for h in range(o_ref.shape[0]):
        y_ref[:, h * vh:(h + 1) * vh] = _dot(o_ref[h], w_ref[h]).astype(y_ref.dtype)


def _uv_proj(o_heads, wuv_t):
    n_heads, rows, _ = o_heads.shape
    vh = wuv_t.shape[2]
    return pl.pallas_call(
        _uv_proj_kernel,
        out_shape=jax.ShapeDtypeStruct((rows, n_heads * vh), BF16),
        compiler_params=pltpu.CompilerParams(vmem_limit_bytes=VMEM_LIMIT),
        name="uv_proj",
    )(o_heads, wuv_t)


def _ret_tail(o, g_raw, gn_g, gn_b):
    mu = jnp.mean(o, axis=-1, keepdims=True)
    d = o - mu
    var = jnp.mean(d * d, axis=-1, keepdims=True)
    gn = d * lax.rsqrt(var + LN_EPS) * gn_g + gn_b
    return g_raw * _sigmoid(g_raw) * gn


def _ret_prompt_kernel(q_ref, k_ref, v_ref, g_ref, cos_ref, sin_ref, intra_ref, qd_ref, kd_ref, cd_ref,
                       gng_ref, gnb_ref, y_ref, s_ref, *, k_scale):
    c = pl.program_id(1)
    hd = s_ref.shape[-1]

    @pl.when(c == 0)
    def _():
        s_ref[...] = jnp.zeros(s_ref.shape, F32)

    cos, sin = cos_ref[...], sin_ref[...]
    for h in range(N_RET_HEADS):
        cols = slice(h * hd, (h + 1) * hd)
        rq = _rope_halves(q_ref[:, cols], cos, sin)
        rk = _rope_halves(k_ref[:, cols], cos, sin) * k_scale
        qb, kb, vb = rq.astype(BF16), rk.astype(BF16), v_ref[:, cols].astype(BF16)
        state = s_ref[0, h]
        scores = _dot_t(qb, kb) * intra_ref[h]
        o = _dot(scores.astype(BF16), vb) + _dot(qb, state.astype(BF16)) * qd_ref[h]
        kd_t = (rk * kd_ref[h]).T.astype(BF16)
        s_ref[0, h] = state * cd_ref[h] + _dot(kd_t, vb)
        y_ref[:, cols] = _ret_tail(o, g_ref[:, cols], gng_ref[:, cols], gnb_ref[:, cols]).astype(y_ref.dtype)


def _ret_prompt(z, cos_r, sin_r, dec, gn_g, gn_b, *, batch, seq, k_scale):
    width = gn_g.shape[1]
    hd = width // N_RET_HEADS
    nc = seq // RET_CHUNK
    intra, qd, kd, cd = dec
    zcol = lambda j: pl.BlockSpec((RET_CHUNK, width), lambda b, c, j=j: (b * nc + c, j))
    const = lambda a: pl.BlockSpec(a.shape, lambda b, c: (0,) * a.ndim)
    return pl.pallas_call(
        functools.partial(_ret_prompt_kernel, k_scale=k_scale),
        grid=(batch, nc),
        in_specs=[zcol(0), zcol(1), zcol(2), zcol(3),
                  pl.BlockSpec((RET_CHUNK, hd // 2), lambda b, c: (b * nc + c, 0)),
                  pl.BlockSpec((RET_CHUNK, hd // 2), lambda b, c: (b * nc + c, 0)),
                  const(intra), const(qd), const(kd), const(cd), const(gn_g), const(gn_b)],
        out_specs=[pl.BlockSpec((RET_CHUNK, width), lambda b, c: (b * nc + c, 0)),
                   pl.BlockSpec((1, N_RET_HEADS, hd, hd), lambda b, c: (b, 0, 0, 0))],
        out_shape=[jax.ShapeDtypeStruct((batch * seq, width), BF16),
                   jax.ShapeDtypeStruct((batch, N_RET_HEADS, hd, hd), F32)],
        compiler_params=_params(2),
        name="ret_prompt",
    )(z, z, z, z, cos_r, sin_r, intra, qd, kd, cd, gn_g, gn_b)


def _ret_sample_kernel(q_ref, k_ref, v_ref, g_ref, cos_ref, sin_ref, st_ref, intra_ref, qd_ref, kd_ref, cd_ref,
                       gng_ref, gnb_ref, prev_ref, y_ref, so_ref, *, k_scale, dec_seq):
    del prev_ref
    bt = so_ref.shape[1]
    hd = so_ref.shape[-1]
    rows = q_ref.shape[0]
    cos, sin = cos_ref[...], sin_ref[...]
    row_b = _div_pow2(lax.broadcasted_iota(jnp.int32, (rows, hd), 0), dec_seq)
    for h in range(N_RET_HEADS):
        cols = slice(h * hd, (h + 1) * hd)
        rq = _rope_halves(q_ref[:, cols], cos, sin)
        rk = _rope_halves(k_ref[:, cols], cos, sin) * k_scale
        qb, kb, vb = rq.astype(BF16), rk.astype(BF16), v_ref[:, cols].astype(BF16)
        scores = _dot_t(qb, kb) * intra_ref[h]
        o = _dot(scores.astype(BF16), vb)
        kd = rk * kd_ref[h]
        for b in range(bt):
            state = st_ref[0, b, h]
            mine = row_b == b
            o = o + jnp.where(mine, _dot(qb, state.astype(BF16)) * qd_ref[h], 0.0)
            kd_t = jnp.where(mine, kd, 0.0).T.astype(BF16)
            so_ref[0, b, h] = state * cd_ref[h] + _dot(kd_t, vb)
        y_ref[:, cols] = _ret_tail(o, g_ref[:, cols], gng_ref[:, cols], gnb_ref[:, cols]).astype(y_ref.dtype)


def _ret_sample(z, cos_r, sin_r, state_all, new_state_all, dec, gn_g, gn_b, *, layer, row0, bs, dec_seq, bt, k_scale):
    width = gn_g.shape[1]
    hd = width // N_RET_HEADS
    rows = bt * dec_seq
    r0 = row0 // rows
    intra, qd, kd, cd = dec
    zcol = lambda j: pl.BlockSpec((rows, width), lambda i, j=j: (r0 + i, j))
    const = lambda a: pl.BlockSpec(a.shape, lambda i: (0,) * a.ndim)
    state_blk = pl.BlockSpec((1, bt, N_RET_HEADS, hd, hd), lambda i: (layer, i, 0, 0, 0))
    in_specs = [zcol(0), zcol(1), zcol(2), zcol(3),
                pl.BlockSpec((rows, hd // 2), lambda i: (r0 + i, 0)),
                pl.BlockSpec((rows, hd // 2), lambda i: (r0 + i, 0)),
                state_blk, const(intra), const(qd), const(kd), const(cd), const(gn_g), const(gn_b),
                pl.BlockSpec(memory_space=pl.ANY)]
    args = [z, z, z, z, cos_r, sin_r, state_all, intra, qd, kd, cd, gn_g, gn_b, new_state_all]
    aliases = {len(args) - 1: 1}
    return pl.pallas_call(
        functools.partial(_ret_sample_kernel, k_scale=k_scale, dec_seq=dec_seq),
        grid=(bs // bt,),
        in_specs=in_specs,
        out_specs=[pl.BlockSpec((rows, width), lambda i: (i, 0)), state_blk],
        out_shape=[jax.ShapeDtypeStruct((bs * dec_seq, width), BF16),
                   jax.ShapeDtypeStruct(state_all.shape, F32)],
        input_output_aliases=aliases,
        compiler_params=_params(1),
        name="ret_sample",
    )(*args)


def _softmax_rows(s):
    e = jnp.exp(s - jnp.max(s, axis=-1, keepdims=True))
    return e / jnp.sum(e, axis=-1, keepdims=True)


def _mem_prompt_kernel(q_ref, kv_ref, o_ref, *, scale):
    width = o_ref.shape[1]
    hd = width // N_MEM_HEADS
    for h in range(N_MEM_HEADS):
        cols = slice(h * hd, (h + 1) * hd)
        k = kv_ref[:, cols].astype(BF16)
        v = kv_ref[:, width + h * hd:width + (h + 1) * hd].astype(BF16)
        p = _softmax_rows(_dot_t(q_ref[:, cols].astype(BF16), k) * scale)
        o_ref[:, cols] = _dot(p.astype(BF16), v).astype(o_ref.dtype)


def _mem_prompt(z, mkv, *, batch, seq, n_mem, width, qcol, tm, scale):
    nt = seq // tm
    return pl.pallas_call(
        functools.partial(_mem_prompt_kernel, scale=scale),
        grid=(batch, nt),
        in_specs=[pl.BlockSpec((tm, width), lambda b, i: (b * nt + i, qcol)),
                  pl.BlockSpec((n_mem, 2 * width), lambda b, i: (b, 0))],
        out_specs=pl.BlockSpec((tm, width), lambda b, i: (b * nt + i, 0)),
        out_shape=jax.ShapeDtypeStruct((batch * seq, width), BF16),
        compiler_params=_params(2),
        name="mem_prompt",
    )(z, mkv)


def _mem_sample_kernel(q_ref, k_ref, v_ref, spread_ref, mask_ref, gather_ref, o_ref, *, scale):
    bt = k_ref.shape[1]
    qb = q_ref[...].astype(BF16)
    mask = mask_ref[...]
    o = jnp.zeros(o_ref.shape, F32)
    for b in range(bt):
        q_heads = (_dot(spread_ref[b], qb) * mask).astype(BF16)
        p = _softmax_rows(_dot_t(q_heads, k_ref[0, b].astype(BF16)) * scale)
        o_heads = _dot(p.astype(BF16), v_ref[0, b].astype(BF16)) * mask
        o = o + _dot(gather_ref[b], o_heads.astype(BF16))
    o_ref[...] = o.astype(o_ref.dtype)


def _mem_sample_tables(bt, dec_seq, width):
    rows = bt * dec_seq
    assert N_MEM_HEADS * dec_seq == rows
    ht = jnp.arange(rows)
    head, tok = ht // dec_seq, ht % dec_seq
    src = jnp.arange(bt)[:, None] * dec_seq + tok[None, :]
    spread = (src[:, :, None] == jnp.arange(rows)[None, None, :]).astype(BF16)
    gather = jnp.swapaxes(spread, 1, 2)
    mask = (jnp.arange(width)[None, :] // (width // N_MEM_HEADS) == head[:, None]).astype(F32)
    return spread, mask, gather


def _mem_sample(z, ck, cv, *, layer, row0, bs, dec_seq, bt, width, qcol, scale):
    rows = bt * dec_seq
    r0 = row0 // rows
    n_mem = ck.shape[2]
    spread, mask, gather = _mem_sample_tables(bt, dec_seq, width)
    cache = pl.BlockSpec((1, bt, n_mem, width), lambda i: (layer, i, 0, 0))
    const = lambda a: pl.BlockSpec(a.shape, lambda i: (0,) * a.ndim)
    return pl.pallas_call(
        functools.partial(_mem_sample_kernel, scale=scale),
        grid=(bs // bt,),
        in_specs=[pl.BlockSpec((rows, width), lambda i: (r0 + i, qcol)), cache, cache,
                  const(spread), const(mask), const(gather)],
        out_specs=pl.BlockSpec((rows, width), lambda i: (i, 0)),
        out_shape=jax.ShapeDtypeStruct((bs * dec_seq, width), BF16),
        compiler_params=_params(1),
        name="mem_sample",
    )(z, ck, cv, spread, mask, gather)


def _merge_kernel(yap_ref, yas_ref, ybp_ref, ybs_ref, ycp_ref, ycs_ref, ga_ref, gb_ref, gc_ref,
                  wa_ref, wb_ref, wc_ref, o_ref, *, prompt_blocks):
    in_prompt = pl.program_id(1) < prompt_blocks
    pick = lambda p_ref, s_ref: jnp.where(in_prompt, p_ref[...], s_ref[...])
    acc = _sigmoid(ga_ref[...]) * _dot(pick(yap_ref, yas_ref), wa_ref[...])
    acc = acc + _sigmoid(gb_ref[...]) * _dot(pick(ybp_ref, ybs_ref), wb_ref[...])
    acc = acc + _sigmoid(gc_ref[...]) * _dot(pick(ycp_ref, ycs_ref), wc_ref[...])
    o_ref[...] = acc.astype(o_ref.dtype)


def _merge(ys_prompt, ys_sample, z, wa, wb, wc, *, layer, gate_col0, tm, tn):
    n_p, width = ys_prompt[0].shape
    n_s = ys_sample[0].shape[0]
    assert n_p % tm == 0 and n_s % tm == 0
    pb = n_p // tm
    d = wa.shape[2]
    nj = d // tn
    g0 = gate_col0 // tn
    y_p = pl.BlockSpec((tm, width), lambda j, i: (jnp.minimum(i, pb - 1), 0))
    y_s = pl.BlockSpec((tm, width), lambda j, i: (jnp.maximum(i - pb, 0), 0))
    w = _layer_block(wa, layer, (width, tn), lambda j, i: (0, j))
    gate = lambda k: pl.BlockSpec((tm, tn), lambda j, i, k=k: (i, g0 + k * nj + j))
    return pl.pallas_call(
        functools.partial(_merge_kernel, prompt_blocks=pb),
        grid=(nj, (n_p + n_s) // tm),
        in_specs=[y_p, y_s, y_p, y_s, y_p, y_s, gate(0), gate(1), gate(2), w, w, w],
        out_specs=pl.BlockSpec((tm, tn), lambda j, i: (i, j)),
        out_shape=jax.ShapeDtypeStruct((n_p + n_s, d), BF16),
        compiler_params=_params(2),
        name="merge",
    )(ys_prompt[0], ys_sample[0], ys_prompt[1], ys_sample[1], ys_prompt[2], ys_sample[2], z, z, z, wa, wb, wc)


def _oproj_ln_kernel(m_ref, w_ref, x_ref, g_ref, b_ref, y_ref, yb_ref, *, alpha):
    y = _layer_norm(alpha * x_ref[...] + _dot(m_ref[...], w_ref[...]), g_ref[...], b_ref[...])
    y_ref[...] = y
    yb_ref[...] = y.astype(BF16)


def _oproj_ln(merged, w_o, x, g, b, *, layer, alpha, tm):
    n, d = x.shape
    row = pl.BlockSpec((tm, d), lambda i: (i, 0))
    const = lambda a: pl.BlockSpec(a.shape, lambda i: (0,) * a.ndim)
    return pl.pallas_call(
        functools.partial(_oproj_ln_kernel, alpha=alpha),
        grid=(n // tm,),
        in_specs=[row, _layer_block(w_o, layer, w_o.shape[1:], lambda i: (0, 0)), row, const(g), const(b)],
        out_specs=[row, row],
        out_shape=[jax.ShapeDtypeStruct((n, d), F32), jax.ShapeDtypeStruct((n, d), BF16)],
        compiler_params=_params(1),
        name="oproj_ln",
    )(merged, w_o, x, g, b)


def _ffn_down_ln_kernel(h_ref, w_ref, x_ref, g_ref, b_ref, y_ref, yb_ref, acc_ref, *, alpha):
    k = pl.program_id(1)
    part = _dot(h_ref[...], w_ref[...])

    @pl.when(k == 0)
    def _():
        acc_ref[...] = part

    @pl.when(k > 0)
    def _():
        acc_ref[...] += part

    @pl.when(k == pl.num_programs(1) - 1)
    def _():
        y = _layer_norm(alpha * x_ref[...] + acc_ref[...], g_ref[...], b_ref[...])
        y_ref[...] = y
        yb_ref[...] = y.astype(BF16)


def _ffn_down_ln(h, w_down, x, g, b, *, layer, alpha, tm, tk):
    n, d = x.shape
    ff = h.shape[1]
    row = pl.BlockSpec((tm, d), lambda i, k: (i, 0))
    const = lambda a: pl.BlockSpec(a.shape, lambda i, k: (0,) * a.ndim)
    return pl.pallas_call(
        functools.partial(_ffn_down_ln_kernel, alpha=alpha),
        grid=(n // tm, ff // tk),
        in_specs=[pl.BlockSpec((tm, tk), lambda i, k: (i, k)),
                  _layer_block(w_down, layer, (tk, d), lambda i, k: (k, 0)),
                  row, const(g), const(b)],
        out_specs=[row, row],
        out_shape=[jax.ShapeDtypeStruct((n, d), F32), jax.ShapeDtypeStruct((n, d), BF16)],
        scratch_shapes=[pltpu.VMEM((tm, d), F32)],
        compiler_params=_params(2),
        name="ffn_down_ln",
    )(h, w_down, x, g, b)


def _rope_tables(pos, rope_d, ret_hd):
    posf = pos.astype(F32)[:, None]

    def cos_sin(d):
        half = d // 2
        inv = jnp.power(ROPE_BASE, -jnp.arange(half, dtype=F32) * (2.0 / d))
        ang = posf * inv[None, :]
        return jnp.cos(ang), jnp.sin(ang)

    cos, sin = cos_sin(rope_d)
    zero = jnp.zeros_like(cos)
    pad = jnp.zeros((cos.shape[0], LANES - 2 * cos.shape[1]), F32)
    mla = (jnp.concatenate([cos, cos, pad], 1),
           jnp.concatenate([-sin, zero, pad], 1),
           jnp.concatenate([zero, sin, pad], 1))
    return mla, cos_sin(ret_hd)


def _decay_tables(log_gamma, chunk, groups, hd):
    idx = jnp.arange(chunk, dtype=F32)
    lg = log_gamma[:, None]
    diff = idx[:, None] - idx[None, :]
    intra = jnp.where(diff >= 0, jnp.exp(lg[:, :, None] * jnp.maximum(diff, 0.0)), 0.0)
    q_decay = jnp.exp(lg * (idx + 1.0))
    k_decay = jnp.exp(lg * (chunk - 1.0 - idx))
    chunk_decay = jnp.exp(log_gamma * chunk)
    nh = log_gamma.shape[0]
    eye = jnp.eye(groups, dtype=F32)
    intra = (eye[None, :, None, :, None] * intra[:, None, :, None, :]).reshape(nh, groups * chunk, groups * chunk)
    q_decay = jnp.broadcast_to(jnp.tile(q_decay, (1, groups))[:, :, None], (nh, groups * chunk, hd))
    k_decay = jnp.broadcast_to(jnp.tile(k_decay, (1, groups))[:, :, None], (nh, groups * chunk, hd))
    chunk_decay = jnp.broadcast_to(chunk_decay[:, None, None], (nh, 1, hd))
    return intra, q_decay, k_decay, chunk_decay


def kernel(x_prompt, x_sample, mem_prompt, cache_mla_latent, cache_mla_rope, page_table, state_ret, cache_mem_k, cache_mem_v, w_in, q_norm_g, kv_norm_g, w_uq, w_uk, w_uv, ret_gn_g, ret_gn_b, w_mem_k, w_mem_v, w_br_mla, w_br_ret, w_br_mem, w_o, ln1_g, ln1_b, w_up, w_down, ln2_g, ln2_b):
    depth = w_in.shape[0]
    batch, seq, d_model = x_prompt.shape
    bs, dec_seq, _ = x_sample.shape
    n_mem = mem_prompt.shape[1]
    kv_lora, n_heads, nope = w_uk.shape[1:]
    q_lora = w_uq.shape[1]
    rope_d = cache_mla_rope.shape[-1]
    ret_width = ret_gn_g.shape[1]
    ret_hd = ret_width // N_RET_HEADS
    mem_width = w_mem_k.shape[2]
    d_ff = w_up.shape[2]
    page = cache_mla_latent.shape[2]
    past = page_table.shape[1] * page
    n_p = batch * seq
    n_s = bs * dec_seq
    n = n_p + n_s
    alpha = (2 * depth) ** 0.25
    mla_scale = (nope + rope_d) ** -0.5
    assert 2 * rope_d <= LANES and (rope_d // 2) * 4 == LANES and ret_hd // 2 == LANES
    off_kv_end = q_lora + kv_lora + rope_d

    memq_col = 4 * ret_width
    gate_col = memq_col + mem_width

    pos = jnp.concatenate([jnp.arange(seq, dtype=jnp.int32)] * batch
                          + [past + jnp.arange(dec_seq, dtype=jnp.int32)] * bs)
    mla_tabs, (cos_r, sin_r) = _rope_tables(pos, rope_d, ret_hd)
    log_gamma = jnp.log(1.0 - jnp.power(2.0, -5.0 - jnp.arange(N_RET_HEADS, dtype=F32)))
    bt = 16 // dec_seq
    dec_p = _decay_tables(log_gamma, RET_CHUNK, 1, ret_hd)
    dec_s = _decay_tables(log_gamma, dec_seq, bt, ret_hd)

    tm_big = _tile(n, 1088)
    tm_mid = _tile(n, 544)
    tm_small = _tile(n, 256)
    tm_prep = _tile(n, 272)
    tm_merge = _tile(math.gcd(n_p, n_s), 512)
    tn = 512 if d_model % 512 == 0 else _tile(d_model, 512, LANES)

    x = jnp.concatenate([x_prompt.reshape(n_p, d_model), x_sample.reshape(n_s, d_model)], axis=0)
    xb = x.astype(BF16)
    mem_b = mem_prompt.reshape(batch * n_mem, d_model).astype(BF16)
    wt_in = jnp.swapaxes(w_in, 1, 2)
    cache_rope_t = jnp.swapaxes(cache_mla_rope, 2, 3)
    pt_flat = page_table.T.reshape(-1)
    z_width = w_in.shape[2] - off_kv_end
    wb_memkv = jnp.concatenate([w_mem_k, w_mem_v], axis=2).astype(BF16)
    wb_mla, wb_ret, wb_mem = w_br_mla.astype(BF16), w_br_ret.astype(BF16), w_br_mem.astype(BF16)
    wb_o, wb_up, wb_down = w_o.astype(BF16), w_up.astype(BF16), w_down.astype(BF16)

    outs = [[] for _ in range(7)]
    new_state_s = jnp.zeros(state_ret.shape, F32)
    for l in range(depth):
        wq = w_uq[l].reshape(q_lora, n_heads, nope + rope_d)
        wq = jnp.pad(wq, ((0, 0), (0, 0), (0, LANES - rope_d))).reshape(q_lora, n_heads * (nope + LANES)).astype(BF16)
        wuk_t = jnp.transpose(w_uk[l], (1, 2, 0)).astype(BF16)
        wuv_t = jnp.transpose(w_uv[l], (1, 0, 2)).astype(BF16)

        z = _matmul_wt(xb, wt_in, F32, layer=l, row0=off_kv_end, n_out=z_width, tm=tm_big,
                       tn=_tile(z_width, 1024, LANES), name="in_proj")
        q_all, kv_all, lat, kro = _mla_prep(xb, wt_in, q_norm_g[l][None], kv_norm_g[l][None], wq, wuk_t, mla_tabs,
                                            layer=l, tm=tm_prep, rope_d=rope_d)

        y_mla_p = _mla_prompt(q_all, kv_all, wuv_t, batch=batch, seq=seq, scale=mla_scale)
        q_s = q_all[:, n_p:].reshape(n_heads, bs, dec_seq, -1).transpose(1, 0, 2, 3).reshape(bs, n_heads * dec_seq, -1)
        kvn_s = jnp.pad(kv_all[n_p:].reshape(bs, dec_seq, -1), ((0, 0), (0, LANES - dec_seq), (0, 0)))
        o_lat_s = _mla_decode(pt_flat, q_s, kvn_s, cache_mla_latent, cache_rope_t, layer=l, scale=mla_scale,
                              dec_seq=dec_seq)
        o_heads = o_lat_s.reshape(bs, n_heads, dec_seq, kv_lora).transpose(1, 0, 2, 3).reshape(n_heads, n_s, kv_lora)
        y_mla_s = _uv_proj(o_heads, wuv_t)

        gn_g, gn_b = ret_gn_g[l][None], ret_gn_b[l][None]
        y_ret_p, s_p = _ret_prompt(z, cos_r, sin_r, dec_p, gn_g, gn_b, batch=batch, seq=seq, k_scale=ret_hd ** -0.5)
        y_ret_s, new_state_s = _ret_sample(z, cos_r, sin_r, state_ret, new_state_s, dec_s, gn_g, gn_b, layer=l,
                                           row0=n_p, bs=bs, dec_seq=dec_seq, bt=bt, k_scale=ret_hd ** -0.5)

        mkv = _matmul(mem_b, wb_memkv, F32, layer=l, tm=_tile(batch * n_mem, 512), tn=tn, name="mem_kv")
        mem_scale = (mem_width // N_MEM_HEADS) ** -0.5
        o_mem_p = _mem_prompt(z, mkv, batch=batch, seq=seq, n_mem=n_mem, width=mem_width,
                              qcol=memq_col // mem_width, tm=_tile(seq, 512), scale=mem_scale)
        o_mem_s = _mem_sample(z, cache_mem_k, cache_mem_v, layer=l, row0=n_p, bs=bs, dec_seq=dec_seq, bt=bt,
                              width=mem_width, qcol=memq_col // mem_width, scale=mem_scale)

        merged = _merge((y_mla_p, y_ret_p, o_mem_p), (y_mla_s, y_ret_s, o_mem_s), z, wb_mla, wb_ret, wb_mem,
                        layer=l, gate_col0=gate_col, tm=tm_merge, tn=_tile(d_model, 1024, LANES))
        x1, x1b = _oproj_ln(merged, wb_o, x, ln1_g[l][None], ln1_b[l][None], layer=l, alpha=alpha, tm=tm_small)
        hmid = _matmul(x1b, wb_up, BF16, layer=l, tm=tm_big, tn=tn, epilogue="relu_sq", name="ffn_up")
        x, xb = _ffn_down_ln(hmid, wb_down, x1, ln2_g[l][None], ln2_b[l][None], layer=l, alpha=alpha,
                             tm=tm_mid, tk=_tile(d_ff, 1024, LANES))

        outs[0].append(lat[:n_p].reshape(batch, seq, kv_lora))
        outs[1].append(kro[:n_p].reshape(batch, seq, rope_d))
        outs[2].append(s_p)
        outs[3].append(mkv[:, :mem_width].reshape(batch, n_mem, mem_width))
        outs[4].append(mkv[:, mem_width:].reshape(batch, n_mem, mem_width))
        outs[5].append(lat[n_p:].reshape(bs, dec_seq, kv_lora))
        outs[6].append(kro[n_p:].reshape(bs, dec_seq, rope_d))

    return (x[:n_p].reshape(batch, seq, d_model), x[n_p:].reshape(bs, dec_seq, d_model),
            *[jnp.stack(o) for o in outs], new_state_s)
```
